```python
import math
import jax, jax.numpy as jnp
from jax import lax
import numpy as np

D_MODEL = 2048
BATCH = 4
SEQ = 8192
DEPTH = 1

CTX_LEN = 256
GRID_W = 64
SSD_EXPAND = 2
D_SSD = SSD_EXPAND * D_MODEL
SSD_HEADDIM = 64
SSD_HEADS = D_SSD // SSD_HEADDIM
SSD_GROUPS = 8
SSD_STATE = 128
SSD_CONV = 5
SSD_CHUNK = 128
GN = SSD_GROUPS * SSD_STATE
D_SC = D_MODEL
SC_CONV = 3
PEER_HEADS = 8
PEER_NKEYS = 128
PEER_EXPERTS = PEER_NKEYS * PEER_NKEYS
PEER_TOPK = 16
PEER_DKEY = 256
PEER_DHALF = PEER_DKEY // 2
PEER_BLOCK = 128
N_ADA = 6
EPS = 1e-6
N_XBC = D_SSD + 2 * GN
OFF_DT = N_XBC
OFF_Z = OFF_DT + 2 * SSD_HEADS
OFF_SC = OFF_Z + D_SSD
OFF_GATE = OFF_SC + 3 * D_SC
N_IN = OFF_GATE + 2 * D_MODEL

kernel_name = "hybrid_ssd_shortconv_peer_dit_block"


def rmsnorm(x, g):
    xf = x.astype(jnp.float32)
    y = xf * lax.rsqrt(jnp.mean(xf * xf, axis=-1, keepdims=True) + EPS)
    return y.astype(x.dtype) * g


def modulate(h, shift, scale):
    return h * (1 + scale) + shift


def rev(t):
    return jnp.flip(t, axis=1)


def ada_chunks(cvec, w, b, n):
    cols = n * D_MODEL
    mod = jax.nn.silu(cvec) @ w[:, :cols] + b[:cols]
    return jnp.split(mod, n, axis=-1)


def dwconv(x, w):
    k_w = w.shape[0]
    pad = k_w // 2
    length = x.shape[-2]
    xp = jnp.pad(x, [(0, 0)] * (x.ndim - 2) + [(pad, pad), (0, 0)])
    acc = w[0] * xp[..., 0:length, :]
    for k in range(1, k_w):
        acc = acc + w[k] * xp[..., k:k + length, :]
    return acc


def ssd_inputs(p_xbc, p_dt, lp):
    b, length, nch = p_xbc.shape
    xbc = jax.nn.silu(dwconv(p_xbc, lp["ssd_conv_w"][:, :nch]) + lp["ssd_conv_b"][:nch])
    xs = xbc[..., :D_SSD].reshape(b, length, SSD_HEADS, SSD_HEADDIM)
    groups = xbc[..., D_SSD:].reshape(b, length, -1, SSD_GROUPS, SSD_STATE)
    dt = jax.nn.softplus(p_dt.astype(jnp.float32) + lp["ssd_dt_bias"].reshape(-1))
    dt = dt.reshape(b, length, 2, SSD_HEADS)
    return xs, groups, dt[:, :, 0], dt[:, :, 1]


def ssd_scan(xs, dt, a_neg, bm, cm, h0):
    b, length, _, _ = xs.shape
    n_chunks = length // SSD_CHUNK
    rg = SSD_HEADS // SSD_GROUPS

    def chunks(t):
        return jnp.moveaxis(t.reshape((b, n_chunks, SSD_CHUNK) + t.shape[2:]), 1, 0)

    xq = chunks(xs.reshape(b, length, SSD_GROUPS, rg, SSD_HEADDIM))
    aq = chunks((dt * a_neg).reshape(b, length, SSD_GROUPS, rg))
    dtq = chunks(dt.reshape(b, length, SSD_GROUPS, rg))
    bq, cq = chunks(bm), chunks(cm)
    tril = jnp.tril(jnp.ones((SSD_CHUNK, SSD_CHUNK), dtype=bool))[:, :, None, None]

    def step(h, inp):
        x_c, a_c, dt_c, b_c, c_c = inp
        acum = jnp.cumsum(a_c, axis=1)
        seg = acum[:, :, None] - acum[:, None, :]
        lmat = jnp.exp(jnp.where(tril, seg, -jnp.inf))
        cb = jnp.einsum("bign,bjgn->bijg", c_c, b_c)
        wts = cb[..., None] * lmat * dt_c[:, None]
        y = jnp.einsum("bijgr,bjgrp->bigrp", wts, x_c)
        y = y + jnp.einsum("bign,bgrpn->bigrp", c_c, h) * jnp.exp(acum)[..., None]
        decay_end = jnp.exp(acum[:, -1:] - acum) * dt_c
        h_new = jnp.exp(acum[:, -1])[..., None, None] * h + jnp.einsum(
            "bjgn,bjgr,bjgrp->bgrpn", b_c, decay_end, x_c)
        return h_new, y

    h_init = h0.reshape(b, SSD_GROUPS, rg, SSD_HEADDIM, SSD_STATE)
    h_fin, y = lax.scan(step, h_init, (xq, aq, dtq, bq, cq))
    y = jnp.moveaxis(y, 0, 1).reshape(b, length, SSD_HEADS, SSD_HEADDIM)
    return y, h_fin.reshape(b, SSD_HEADS, SSD_HEADDIM, SSD_STATE)


def ssd_final_state(xs, dt, a_neg, bm):
    b, length, _, _ = xs.shape
    rg = SSD_HEADS // SSD_GROUPS
    acum = jnp.cumsum(dt * a_neg, axis=1)
    wts = (jnp.exp(acum[:, -1:] - acum) * dt).reshape(b, length, SSD_GROUPS, rg)
    hs = jnp.einsum("blgn,blgr,blgrp->bgrpn", bm, wts,
                    xs.reshape(b, length, SSD_GROUPS, rg, SSD_HEADDIM))
    return hs.reshape(b, SSD_HEADS, SSD_HEADDIM, SSD_STATE)


def token_mixer(h, lp, h0_f, h0_b, on_grid):
    b, length, _ = h.shape
    w_in = lp["w_in"]
    xs, groups, dt_f, dt_b = ssd_inputs(h @ w_in[:, :N_XBC], h @ w_in[:, OFF_DT:OFF_Z], lp)
    bm, cm = groups[:, :, 0], groups[:, :, 1]
    a_neg = lp["A"]
    y_f, h_f = ssd_scan(xs, dt_f, a_neg[0], bm, cm, h0_f)
    y_b, h_b = ssd_scan(rev(xs), rev(dt_b), a_neg[1], rev(bm), rev(cm), h0_b)
    y = y_f + rev(y_b) + lp["ssd_D"][:, None] * xs
    z = h @ w_in[:, OFF_Z:OFF_SC]
    y = rmsnorm(y.reshape(b, length, D_SSD) * jax.nn.silu(z), lp["ssd_norm_g"])
    y_ssd = y @ lp["ssd_w_out"]
    sc_b, sc_c, sc_x = jnp.split(h @ w_in[:, OFF_SC:OFF_GATE], 3, axis=-1)
    u = sc_c * sc_x
    if on_grid:
        rows = length // GRID_W
        u = dwconv(u.reshape(b, rows, GRID_W, D_SC), lp["sc_conv_w"]).reshape(b, length, D_SC)
    else:
        u = dwconv(u, lp["sc_conv_w"])
    y_sc = (sc_b * u) @ lp["sc_w_out"]
    g_ssd, g_sc = jnp.split(jax.nn.sigmoid(h @ w_in[:, OFF_GATE:]), 2, axis=-1)
    out = (g_ssd * y_ssd + g_sc * y_sc) @ lp["w_o"]
    return out, h_f, h_b


def peer_ffn(h, lp):
    b, length, d = h.shape
    blocks = h.reshape(b * length // PEER_BLOCK, PEER_BLOCK, d)

    def one_block(hb):
        q = (hb @ lp["peer_w_q"]).reshape(PEER_BLOCK, PEER_HEADS, 2, PEER_DHALF)
        s = jnp.einsum("thsk,hsnk->thsn", q, lp["peer_keys"])
        sv, si = lax.top_k(s, PEER_TOPK)
        cand_s = (sv[:, :, 0, :, None] + sv[:, :, 1, None, :]).reshape(PEER_BLOCK, PEER_HEADS, -1)
        cand_i = (si[:, :, 0, :, None] * PEER_NKEYS + si[:, :, 1, None, :]).reshape(PEER_BLOCK, PEER_HEADS, -1)
        top_s, pos = lax.top_k(cand_s, PEER_TOPK)
        idx = jnp.take_along_axis(cand_i, pos, axis=-1)
        gate = jax.nn.softmax(top_s.astype(jnp.float32), axis=-1)
        u = lp["peer_u"][idx]
        act = jax.nn.gelu(jnp.einsum("td,thkd->thk", hb, u), approximate=False)
        return jnp.einsum("thk,thkd->td", (gate * act).astype(hb.dtype), lp["peer_v"][idx])

    return lax.map(one_block, blocks).reshape(b, length, d)


def setup_inputs(seed: int = 0) -> dict:
    key = jax.random.key(seed)
    ks = jax.random.split(key, 24)

    def nrm(k, shape, s):
        return jax.random.normal(k, shape, jnp.float32) * s

    dt0 = jnp.exp(jax.random.uniform(ks[10], (DEPTH, 2, SSD_HEADS), jnp.float32,
                                     math.log(1e-3), math.log(1e-1)))
    return {
        "x": nrm(ks[0], (BATCH, SEQ, D_MODEL), 1.0),
        "c": nrm(ks[1], (BATCH, D_MODEL), 1.0),
        "ctx": nrm(ks[2], (BATCH, CTX_LEN, D_MODEL), 1.0),
        "c_ctx": nrm(ks[3], (D_MODEL,), 1.0),
        "w_ada": nrm(ks[4], (DEPTH, D_MODEL, N_ADA * D_MODEL), D_MODEL ** -0.5),
        "b_ada": nrm(ks[5], (DEPTH, N_ADA * D_MODEL), 0.02),
        "norm1_g": 1.0 + nrm(ks[6], (DEPTH, D_MODEL), 0.02),
        "norm2_g": 1.0 + nrm(ks[7], (DEPTH, D_MODEL), 0.02),
        "w_in": nrm(ks[8], (DEPTH, D_MODEL, N_IN), D_MODEL ** -0.5),
        "ssd_conv_w": nrm(ks[9], (DEPTH, SSD_CONV, N_XBC), SSD_CONV ** -0.5),
        "ssd_conv_b": nrm(ks[11], (DEPTH, N_XBC), 0.02),
        "ssd_dt_bias": dt0 + jnp.log(-jnp.expm1(-dt0)),
        "ssd_A_log": jnp.log(jax.random.uniform(ks[12], (DEPTH, 2, SSD_HEADS), jnp.float32, 1.0, 16.0)),
        "ssd_D": 1.0 + nrm(ks[13], (DEPTH, SSD_HEADS), 0.02),
        "ssd_norm_g": 1.0 + nrm(ks[14], (DEPTH, D_SSD), 0.02),
        "ssd_w_out": nrm(ks[15], (DEPTH, D_SSD, D_MODEL), D_SSD ** -0.5),
        "sc_conv_w": nrm(ks[16], (DEPTH, SC_CONV, D_SC), SC_CONV ** -0.5),
        "sc_w_out": nrm(ks[17], (DEPTH, D_SC, D_MODEL), D_SC ** -0.5),
        "w_o": nrm(ks[18], (DEPTH, D_MODEL, D_MODEL), D_MODEL ** -0.5),
        "peer_w_q": nrm(ks[19], (DEPTH, D_MODEL, PEER_HEADS * PEER_DKEY), D_MODEL ** -0.5),
        "peer_keys": nrm(ks[20], (DEPTH, PEER_HEADS, 2, PEER_NKEYS, PEER_DHALF), PEER_DHALF ** -0.5),
        "peer_u": nrm(ks[21], (DEPTH, PEER_EXPERTS, D_MODEL), D_MODEL ** -0.5),
        "peer_v": nrm(ks[22], (DEPTH, PEER_EXPERTS, D_MODEL), PEER_HEADS ** -0.5),
        "final_g": 1.0 + nrm(ks[23], (D_MODEL,), 0.02),
    }


def reference(x, c, ctx, c_ctx, w_ada, b_ada, norm1_g, norm2_g, w_in, ssd_conv_w, ssd_conv_b,
              ssd_dt_bias, ssd_A_log, ssd_D, ssd_norm_g, ssd_w_out, sc_conv_w, sc_w_out, w_o,
              peer_w_q, peer_keys, peer_u, peer_v, final_g):
    a_all = -jnp.exp(ssd_A_log.astype(jnp.float32))
    for i in range(DEPTH):
        lp = {"w_in": w_in[i], "ssd_conv_w": ssd_conv_w[i], "ssd_conv_b": ssd_conv_b[i],
              "ssd_dt_bias": ssd_dt_bias[i], "A": a_all[i], "ssd_D": ssd_D[i],
              "ssd_norm_g": ssd_norm_g[i], "ssd_w_out": ssd_w_out[i], "sc_conv_w": sc_conv_w[i],
              "sc_w_out": sc_w_out[i], "w_o": w_o[i], "peer_w_q": peer_w_q[i],
              "peer_keys": peer_keys[i], "peer_u": peer_u[i], "peer_v": peer_v[i]}
        last = i == DEPTH - 1
        if last:
            sh_c1, sc_c1 = ada_chunks(c_ctx, w_ada[i], b_ada[i], 2)
            hc = modulate(rmsnorm(ctx, norm1_g[i]), sh_c1, sc_c1)
            xs_c, grp_c, dtf_c, dtb_c = ssd_inputs(hc @ w_in[i][:, :D_SSD + GN],
                                                   hc @ w_in[i][:, OFF_DT:OFF_Z], lp)
            b_c = grp_c[:, :, 0]
            hf_c = ssd_final_state(xs_c, dtf_c, lp["A"][0], b_c)
            hb_c = ssd_final_state(rev(xs_c), rev(dtb_c), lp["A"][1], rev(b_c))
        else:
            sh_c1, sc_c1, g_c1, sh_c2, sc_c2, g_c2 = ada_chunks(c_ctx, w_ada[i], b_ada[i], N_ADA)
            hc = modulate(rmsnorm(ctx, norm1_g[i]), sh_c1, sc_c1)
            zero = jnp.zeros((ctx.shape[0], SSD_HEADS, SSD_HEADDIM, SSD_STATE), jnp.float32)
            out_c, hf_c, hb_c = token_mixer(hc, lp, zero, zero, False)
        sh1, sc1, g1, sh2, sc2, g2 = [m[:, None, :] for m in ada_chunks(c, w_ada[i], b_ada[i], N_ADA)]
        h = modulate(rmsnorm(x, norm1_g[i]), sh1, sc1)
        out, _, _ = token_mixer(h, lp, hf_c, hb_c, True)
        x = x + g1 * out
        x = x + g2 * peer_ffn(modulate(rmsnorm(x, norm2_g[i]), sh2, sc2), lp)
        if not last:
            ctx = ctx + g_c1 * out_c
            ctx = ctx + g_c2 * peer_ffn(modulate(rmsnorm(ctx, norm2_g[i]), sh_c2, sc_c2), lp)
    return rmsnorm(x, final_g)
```

```python
import functools
import math

import jax
import jax.numpy as jnp
from jax import lax
from jax.experimental import pallas as pl
from jax.experimental.pallas import tpu as pltpu

F32 = jnp.float32
BF16 = jnp.bfloat16

LANE = 128
BF16_ROWS = 16
VMEM_LIMIT_BYTES = 56 * 1024 * 1024

EPS = 1e-6
SSD_CHUNK = 128
SSD_STATE = 128
GRID_W = 64
PEER_TOPK = 16
NEG = -1e30

HIGHEST = lax.Precision.HIGHEST


def _cparams(n_axes):
    return pltpu.CompilerParams(dimension_semantics=("arbitrary",) * n_axes,
                                vmem_limit_bytes=VMEM_LIMIT_BYTES)


def _resident(block_shape, index_map):
    return pl.BlockSpec(block_shape, index_map, pipeline_mode=pl.Buffered(1))


def _silu(v):
    return v * jax.nn.sigmoid(v)


def _rms(v, g):
    return v * lax.rsqrt(jnp.mean(v * v, axis=-1, keepdims=True) + EPS) * g


def _nt_dot(a, b):
    return lax.dot_general(a, b, (((1,), (1,)), ((), ())), preferred_element_type=F32)


def _ada_kernel(c_ref, w_ref, b_ref, o_ref):
    s = _silu(c_ref[...])
    o_ref[...] = jnp.dot(s, w_ref[...], preferred_element_type=F32, precision=HIGHEST) + b_ref[...]


def _ada(cvec, w, b):
    rows, d = cvec.shape
    n = w.shape[1]
    tn = 1024 if n % 1024 == 0 else d
    return pl.pallas_call(
        _ada_kernel,
        grid=(n // tn,),
        in_specs=[pl.BlockSpec((rows, d), lambda j: (0, 0)),
                  pl.BlockSpec((d, tn), lambda j: (0, j)),
                  pl.BlockSpec((1, tn), lambda j: (0, j))],
        out_specs=pl.BlockSpec((rows, tn), lambda j: (0, j)),
        out_shape=jax.ShapeDtypeStruct((rows, n), F32),
        compiler_params=_cparams(1),
        name="ada",
    )(cvec, w, b.reshape(1, n))


def _inproj_kernel(x_ref, g_ref, sh_ref, sc_ref, w_ref, wdt_ref, p_ref, dt_ref, h_scr):
    @pl.when(pl.program_id(1) == 0)
    def _():
        h = _rms(x_ref[...], g_ref[...]) * (1.0 + sc_ref[0]) + sh_ref[0]
        hb = h.astype(BF16)
        h_scr[...] = hb
        dt_ref[...] = jnp.dot(hb, wdt_ref[...], preferred_element_type=F32)

    p_ref[...] = jnp.dot(h_scr[...], w_ref[...], preferred_element_type=F32).astype(p_ref.dtype)


def _inproj(x2d, norm_g, shift, scale, w_main, w_dt, seq_len):
    t, d = x2d.shape
    n = w_main.shape[1]
    tm = min(1024, seq_len)
    tn = 1024 if n % 1024 == 0 else d
    per_batch = seq_len // tm
    return pl.pallas_call(
        _inproj_kernel,
        grid=(t // tm, n // tn),
        in_specs=[pl.BlockSpec((tm, d), lambda i, j: (i, 0)),
                  pl.BlockSpec((1, d), lambda i, j: (0, 0)),
                  pl.BlockSpec((1, 1, d), lambda i, j: (i // per_batch, 0, 0)),
                  pl.BlockSpec((1, 1, d), lambda i, j: (i // per_batch, 0, 0)),
                  pl.BlockSpec((d, tn), lambda i, j: (0, j)),
                  pl.BlockSpec((d, LANE), lambda i, j: (0, 0))],
        out_specs=[pl.BlockSpec((tm, tn), lambda i, j: (i, j)),
                   pl.BlockSpec((tm, LANE), lambda i, j: (i, 0))],
        out_shape=[jax.ShapeDtypeStruct((t, n), BF16),
                   jax.ShapeDtypeStruct((t, LANE), F32)],
        scratch_shapes=[pltpu.VMEM((tm, d), BF16)],
        compiler_params=_cparams(2),
        name="inproj",
    )(x2d, norm_g.reshape(1, d), shift, scale, w_main, w_dt)


def _conv_kernel(main_ref, prev_ref, next_ref, w_ref, b_ref, o_ref, e_scr, *, rows, n_rb, taps):
    rb = pl.program_id(1)
    halo = BF16_ROWS
    pad = taps // 2
    e_scr[halo:halo + rows, :] = main_ref[...].astype(F32)
    e_scr[0:halo, :] = jnp.where(rb > 0, prev_ref[...].astype(F32), 0.0)
    e_scr[halo + rows:2 * halo + rows, :] = jnp.where(rb < n_rb - 1, next_ref[...].astype(F32), 0.0)
    acc = b_ref[...] + w_ref[0:1, :] * e_scr[halo - pad:halo - pad + rows, :]
    for k in range(1, taps):
        acc = acc + w_ref[k:k + 1, :] * e_scr[halo - pad + k:halo - pad + k + rows, :]
    act = _silu(acc).astype(o_ref.dtype)
    for t in range(o_ref.shape[1]):
        o_ref[0, t] = act[:, t * LANE:(t + 1) * LANE]


def _ssd_conv(p2d, conv_w, conv_b, batch, seq_len, n_xbc, tc, col_blk):
    taps = conv_w.shape[0]
    rows = min(512, seq_len)
    n_rb = seq_len // rows
    n_col = n_xbc // tc
    halo = BF16_ROWS
    blocks_per_rb = rows // halo
    last_halo_block = batch * seq_len // halo - 1
    kern = functools.partial(_conv_kernel, rows=rows, n_rb=n_rb, taps=taps)
    return pl.pallas_call(
        kern,
        grid=(batch, n_rb, n_col),
        in_specs=[
            pl.BlockSpec((rows, tc), lambda b, r, j: (b * n_rb + r, j + col_blk)),
            pl.BlockSpec((halo, tc),
                         lambda b, r, j: (jnp.maximum((b * n_rb + r) * blocks_per_rb - 1, 0), j + col_blk)),
            pl.BlockSpec((halo, tc),
                         lambda b, r, j: (jnp.minimum((b * n_rb + r + 1) * blocks_per_rb, last_halo_block),
                                          j + col_blk)),
            pl.BlockSpec((taps, tc), lambda b, r, j: (0, j)),
            pl.BlockSpec((1, tc), lambda b, r, j: (0, j)),
        ],
        out_specs=pl.BlockSpec((1, tc // LANE, rows, LANE), lambda b, r, j: (b, j, r, 0)),
        out_shape=jax.ShapeDtypeStruct((batch, n_xbc // LANE, seq_len, LANE), BF16),
        scratch_shapes=[pltpu.VMEM((rows + 2 * halo, tc), F32)],
        compiler_params=_cparams(3),
        name="ssdconv",
    )(p2d, p2d, p2d, conv_w, conv_b.reshape(1, n_xbc))


def _ssd_kernel(*refs, reverse, add_prev, n_chunks, heads, groups):
    if add_prev:
        (x3_ref, dtr_ref, bias_ref, alog_ref, h0_ref, drow_ref, yprev_ref,
         y_ref, hfin_ref, h_scr) = refs
    else:
        x3_ref, dtr_ref, bias_ref, alog_ref, h0_ref, y_ref, hfin_ref, h_scr = refs
    q = SSD_CHUNK
    n_pairs = heads // 2
    pairs_per_group = n_pairs // groups
    col0 = heads if reverse else 0
    step = pl.program_id(1)

    @pl.when(step == 0)
    def _():
        h_scr[...] = h0_ref[0]

    pre = dtr_ref[0] + bias_ref[...]
    dt = jnp.maximum(pre, 0.0) + jnp.log1p(jnp.exp(-jnp.abs(pre)))
    a = dt * (-jnp.exp(alog_ref[...]))
    ri = lax.broadcasted_iota(jnp.int32, (q, q), 0)
    ci = lax.broadcasted_iota(jnp.int32, (q, q), 1)
    tri = (ci >= ri) if reverse else (ci <= ri)
    tri_t = (ri >= ci) if reverse else (ri <= ci)
    acum = jnp.dot(tri.astype(F32), a, preferred_element_type=F32, precision=HIGHEST)
    a_t = a.T
    dt_t = dt.T
    acum_t = jnp.dot(a_t, tri_t.astype(F32), preferred_element_type=F32, precision=HIGHEST)
    tot = jnp.dot(a_t, jnp.ones((q, q), F32), preferred_element_type=F32, precision=HIGHEST)
    w_t = dt_t * jnp.exp(tot - acum_t)
    dec_t = jnp.exp(tot)
    low_lanes = ci < (LANE // 2)
    low_rows = ri < (LANE // 2)

    cb = None
    for kp in range(n_pairs):
        g = kp // pairs_per_group
        b_g = x3_ref[0, n_pairs + g]
        c_g = x3_ref[0, n_pairs + groups + g]
        if kp % pairs_per_group == 0:
            cb = _nt_dot(c_g, b_g)
            c_f = c_g.astype(F32)
        ms, cds = [], []
        for hc in (col0 + 2 * kp, col0 + 2 * kp + 1):
            col = jnp.broadcast_to(acum[:, hc:hc + 1], (q, q))
            seg = col - acum_t[hc:hc + 1, :]
            decay = jnp.exp(jnp.where(tri, seg, NEG))
            ms.append((cb * decay * dt_t[hc:hc + 1, :]).astype(BF16))
            cds.append((c_f * jnp.exp(col)).astype(BF16))
        xp = x3_ref[0, kp].astype(F32)
        rhs = jnp.concatenate([jnp.where(low_lanes, xp, 0.0), jnp.where(low_lanes, 0.0, xp)],
                              axis=0).astype(BF16)
        y = jnp.dot(jnp.concatenate(ms, axis=1), rhs, preferred_element_type=F32)
        hp = h_scr[kp]
        rhs2 = jnp.concatenate([jnp.where(low_rows, hp, 0.0), jnp.where(low_rows, 0.0, hp)],
                               axis=1).astype(BF16)
        y = y + _nt_dot(jnp.concatenate(cds, axis=1), rhs2)
        if add_prev:
            y = y + drow_ref[kp:kp + 1, :] * xp + yprev_ref[0, :, kp * LANE:(kp + 1) * LANE].astype(F32)
        y_ref[0, :, kp * LANE:(kp + 1) * LANE] = y.astype(y_ref.dtype)
        h1, h2 = col0 + 2 * kp, col0 + 2 * kp + 1
        w_sel = jnp.where(low_rows, w_t[h1:h1 + 1, :], w_t[h2:h2 + 1, :])
        d_sel = jnp.where(low_rows, dec_t[h1:h1 + 1, :], dec_t[h2:h2 + 1, :])
        xw_t = (xp.T * w_sel).astype(BF16)
        h_scr[kp] = d_sel * hp + jnp.dot(xw_t, b_g, preferred_element_type=F32)

    @pl.when(step == n_chunks - 1)
    def _():
        hfin_ref[0] = h_scr[...]


def _ssd_scan(x3, dtraw, dt_bias, a_log, h0, heads, groups, reverse, drow=None, yprev=None):
    batch, n_tiles, seq_len, _ = x3.shape
    n_chunks = seq_len // SSD_CHUNK
    n_pairs = heads // 2
    d_ssd = n_pairs * LANE
    add_prev = yprev is not None
    if reverse:
        chunk = lambda k: n_chunks - 1 - k
    else:
        chunk = lambda k: k
    in_specs = [
        pl.BlockSpec((1, n_tiles, SSD_CHUNK, LANE), lambda b, k: (b, 0, chunk(k), 0)),
        pl.BlockSpec((1, SSD_CHUNK, LANE), lambda b, k: (b, chunk(k), 0)),
        pl.BlockSpec((1, LANE), lambda b, k: (0, 0)),
        pl.BlockSpec((1, LANE), lambda b, k: (0, 0)),
        pl.BlockSpec((1, n_pairs, LANE, SSD_STATE), lambda b, k: (b, 0, 0, 0)),
    ]
    args = [x3, dtraw, dt_bias, a_log, h0]
    if add_prev:
        in_specs += [pl.BlockSpec((n_pairs, LANE), lambda b, k: (0, 0)),
                     pl.BlockSpec((1, SSD_CHUNK, d_ssd), lambda b, k: (b, chunk(k), 0))]
        args += [drow, yprev]
    kern = functools.partial(_ssd_kernel, reverse=reverse, add_prev=add_prev, n_chunks=n_chunks,
                             heads=heads, groups=groups)
    return pl.pallas_call(
        kern,
        grid=(batch, n_chunks),
        in_specs=in_specs,
        out_specs=[pl.BlockSpec((1, SSD_CHUNK, d_ssd), lambda b, k: (b, chunk(k), 0)),
                   pl.BlockSpec((1, n_pairs, LANE, SSD_STATE), lambda b, k: (b, 0, 0, 0))],
        out_shape=[jax.ShapeDtypeStruct((batch, seq_len, d_ssd), BF16),
                   jax.ShapeDtypeStruct((batch, n_pairs, LANE, SSD_STATE), F32)],
        scratch_shapes=[pltpu.VMEM((n_pairs, LANE, SSD_STATE), F32)],
        compiler_params=_cparams(2),
        name="ssd_bwd" if reverse else "ssd_fwd",
    )(*args)


def _mixa_kernel(y_ref, z_ref, gate_ref, ng_ref, w_ref, o_ref):
    z = z_ref[...].astype(F32)
    v = _rms(y_ref[...].astype(F32) * _silu(z), ng_ref[...])
    o = jnp.dot(v.astype(BF16), w_ref[...], preferred_element_type=F32)
    o_ref[...] = (jax.nn.sigmoid(gate_ref[...].astype(F32)) * o).astype(o_ref.dtype)


def _mixa(y2d, p2d, norm_g, w_out, gate_blk):
    t, d_ssd = y2d.shape
    d = w_out.shape[1]
    tm = min(256, t)
    return pl.pallas_call(
        _mixa_kernel,
        grid=(t // tm,),
        in_specs=[pl.BlockSpec((tm, d_ssd), lambda i: (i, 0)),
                  pl.BlockSpec((tm, d_ssd), lambda i: (i, 0)),
                  pl.BlockSpec((tm, d), lambda i: (i, gate_blk)),
                  pl.BlockSpec((1, d_ssd), lambda i: (0, 0)),
                  _resident((d_ssd, d), lambda i: (0, 0))],
        out_specs=pl.BlockSpec((tm, d), lambda i: (i, 0)),
        out_shape=jax.ShapeDtypeStruct((t, d), BF16),
        compiler_params=_cparams(1),
        name="mixa",
    )(y2d, p2d, p2d, norm_g.reshape(1, d_ssd), w_out)


def _mixb_kernel(scb_ref, scc_ref, scx_ref, gsc_ref, m1_ref, x_ref, g1_ref, cw_ref, wsc_ref, wo_ref,
                 n2_ref, sh2_ref, sc2_ref, x1_ref, hq_ref, u_scr, *, rows):
    pad = 8
    u_scr[0:pad, :] = jnp.zeros((pad, u_scr.shape[1]), F32)
    u_scr[pad + rows:2 * pad + rows, :] = jnp.zeros((pad, u_scr.shape[1]), F32)
    u_scr[pad:pad + rows, :] = scc_ref[...].astype(F32) * scx_ref[...].astype(F32)
    pos = jnp.bitwise_and(lax.broadcasted_iota(jnp.int32, (rows, u_scr.shape[1]), 0), GRID_W - 1)
    left = jnp.where(pos == 0, 0.0, u_scr[pad - 1:pad - 1 + rows, :])
    right = jnp.where(pos == GRID_W - 1, 0.0, u_scr[pad + 1:pad + 1 + rows, :])
    conv = cw_ref[0:1, :] * left + cw_ref[1:2, :] * u_scr[pad:pad + rows, :] + cw_ref[2:3, :] * right
    v = (scb_ref[...].astype(F32) * conv).astype(BF16)
    y_sc = jnp.dot(v, wsc_ref[...], preferred_element_type=F32)
    merged = m1_ref[...].astype(F32) + jax.nn.sigmoid(gsc_ref[...].astype(F32)) * y_sc
    out = jnp.dot(merged.astype(BF16), wo_ref[...], preferred_element_type=F32)
    x1 = x_ref[...] + g1_ref[0] * out
    x1_ref[...] = x1
    hq_ref[...] = (_rms(x1, n2_ref[...]) * (1.0 + sc2_ref[0]) + sh2_ref[0]).astype(hq_ref.dtype)


def _mixb(p2d, m1, x2d, g1, sc_conv_w, w_sc, w_o, norm2_g, sh2, sc2, seq_len, scb_blk, gsc_blk):
    t, d = x2d.shape
    tm = min(256, t)
    per_batch = seq_len // tm
    bmap = lambda i: (i // per_batch, 0, 0)
    tile = lambda blk: pl.BlockSpec((tm, d), lambda i: (i, blk))
    kern = functools.partial(_mixb_kernel, rows=tm)
    return pl.pallas_call(
        kern,
        grid=(t // tm,),
        in_specs=[tile(scb_blk), tile(scb_blk + 1), tile(scb_blk + 2), tile(gsc_blk),
                  tile(0), tile(0),
                  pl.BlockSpec((1, 1, d), bmap),
                  pl.BlockSpec(sc_conv_w.shape, lambda i: (0, 0)),
                  _resident((d, d), lambda i: (0, 0)),
                  _resident((d, d), lambda i: (0, 0)),
                  pl.BlockSpec((1, d), lambda i: (0, 0)),
                  pl.BlockSpec((1, 1, d), bmap),
                  pl.BlockSpec((1, 1, d), bmap)],
        out_specs=[tile(0), tile(0)],
        out_shape=[jax.ShapeDtypeStruct((t, d), F32), jax.ShapeDtypeStruct((t, d), BF16)],
        scratch_shapes=[pltpu.VMEM((tm + 16, d), F32)],
        compiler_params=_cparams(1),
        name="mixb",
    )(p2d, p2d, p2d, p2d, m1, x2d, g1, sc_conv_w, w_sc, w_o, norm2_g.reshape(1, d), sh2, sc2)


def _top_values(cur, n, sv_ref):
    for it in range(n):
        m = jnp.max(cur, axis=0, keepdims=True)
        sv_ref[it:it + 1, :] = m
        cur = jnp.where(cur >= m, NEG, cur)


def _peera_kernel(hq_ref, wq_ref, keys_ref, thr_ref, coef_ref, s2_ref, e2_ref,
                  sv1_scr, sv2_scr, cand_scr, svc_scr):
    k = PEER_TOPK
    n_heads = keys_ref.shape[0]
    dk = keys_ref.shape[3]
    q_t = _nt_dot(wq_ref[...], hq_ref[...])
    for h in range(n_heads):
        s_t = []
        for s in range(2):
            r0 = (2 * h + s) * dk
            s_t.append(jnp.dot(keys_ref[h, s], q_t[r0:r0 + dk, :].astype(BF16),
                               preferred_element_type=F32))
        _top_values(s_t[0], k + 1, sv1_scr)
        _top_values(s_t[1], k + 1, sv2_scr)
        sv2 = sv2_scr[0:k, :]
        for a in range(k):
            cand_scr[a * k:(a + 1) * k, :] = sv1_scr[a:a + 1, :] + sv2
        cand = cand_scr[...]
        _top_values(cand, k + 1, svc_scr)
        c_next = jnp.maximum(svc_scr[k:k + 1, :],
                             jnp.maximum(sv1_scr[k:k + 1, :] + sv2_scr[0:1, :],
                                         sv1_scr[0:1, :] + sv2_scr[k:k + 1, :]))
        tau = 0.5 * (svc_scr[k - 1:k, :] + c_next)
        top = sv1_scr[0:1, :] + sv2_scr[0:1, :]
        z = jnp.sum(jnp.where(cand > tau, jnp.exp(cand - top), 0.0), axis=0, keepdims=True)
        thr_ref[h] = tau - s_t[0]
        coef_ref[h] = jnp.exp(s_t[0] - sv1_scr[0:1, :]) / z
        s2_ref[h] = s_t[1]
        e2_ref[h] = jnp.exp(s_t[1] - sv2_scr[0:1, :])


def _peera(hq, wq_t, keys):
    t, d = hq.shape
    n_heads, _, n_keys, _ = keys.shape
    tt = min(512, t)
    k = PEER_TOPK
    out = jax.ShapeDtypeStruct((n_heads, n_keys, t), F32)
    ospec = pl.BlockSpec((n_heads, n_keys, tt), lambda i: (0, 0, i))
    return pl.pallas_call(
        _peera_kernel,
        grid=(t // tt,),
        in_specs=[pl.BlockSpec((tt, d), lambda i: (i, 0)),
                  _resident(wq_t.shape, lambda i: (0, 0)),
                  pl.BlockSpec(keys.shape, lambda i: (0, 0, 0, 0))],
        out_specs=[ospec] * 4,
        out_shape=[out] * 4,
        scratch_shapes=[pltpu.VMEM((k + 8, tt), F32), pltpu.VMEM((k + 8, tt), F32),
                        pltpu.VMEM((k * k, tt), F32), pltpu.VMEM((k + 8, tt), F32)],
        compiler_params=_cparams(1),
        name="peera",
    )(hq, wq_t, keys)


def _peerb_kernel(hq_ref, thr_ref, coef_ref, s2_ref, e2_ref, u_ref, vt_ref, x1_ref, g2_ref, fg_ref,
                  o_ref, acc_scr, wg_scr):
    eb = pl.program_id(1)
    n_heads, n_keys, _ = s2_ref.shape

    @pl.when(eb == 0)
    def _():
        acc_scr[...] = jnp.zeros_like(acc_scr)

    a_t = _nt_dot(u_ref[...], hq_ref[...])
    for ii in range(thr_ref.shape[1]):
        rows = slice(ii * n_keys, (ii + 1) * n_keys)
        w = None
        for h in range(n_heads):
            term = coef_ref[h, ii:ii + 1, :] * jnp.where(s2_ref[h] >= thr_ref[h, ii:ii + 1, :],
                                                         e2_ref[h], 0.0)
            w = term if w is None else w + term
        act = a_t[rows, :]
        gelu = 0.5 * act * (1.0 + lax.erf(act * (1.0 / math.sqrt(2.0))))
        wg_scr[rows, :] = (w * gelu).astype(BF16)
    acc_scr[...] += jnp.dot(vt_ref[...], wg_scr[...], preferred_element_type=F32)

    @pl.when(eb == pl.num_programs(1) - 1)
    def _():
        x2 = x1_ref[...] + g2_ref[0] * acc_scr[...].T
        o_ref[...] = _rms(x2, fg_ref[...])


def _peerb(hq, thr, coef, s2, e2, u, v_t, x1, g2, final_g, seq_len):
    t, d = hq.shape
    n_heads, n_keys, _ = s2.shape
    n_exp = u.shape[0]
    tt = min(512, t)
    i_blk = 8
    et = i_blk * n_keys
    per_batch = seq_len // tt
    stat = pl.BlockSpec((n_heads, n_keys, tt), lambda i, e: (0, 0, i), pipeline_mode=pl.Buffered(1))
    rowstat = pl.BlockSpec((n_heads, i_blk, tt), lambda i, e: (0, e, i))
    return pl.pallas_call(
        _peerb_kernel,
        grid=(t // tt, n_exp // et),
        in_specs=[pl.BlockSpec((tt, d), lambda i, e: (i, 0)),
                  rowstat, rowstat, stat, stat,
                  pl.BlockSpec((et, d), lambda i, e: (e, 0)),
                  pl.BlockSpec((d, et), lambda i, e: (0, e)),
                  pl.BlockSpec((tt, d), lambda i, e: (i, 0), pipeline_mode=pl.Buffered(1)),
                  pl.BlockSpec((1, 1, d), lambda i, e: (i // per_batch, 0, 0)),
                  pl.BlockSpec((1, d), lambda i, e: (0, 0))],
        out_specs=pl.BlockSpec((tt, d), lambda i, e: (i, 0)),
        out_shape=jax.ShapeDtypeStruct((t, d), F32),
        scratch_shapes=[pltpu.VMEM((d, tt), F32), pltpu.VMEM((et, tt), BF16)],
        compiler_params=_cparams(2),
        name="peerb",
    )(hq, thr, coef, s2, e2, u, v_t, x1, g2, final_g.reshape(1, d))


def kernel(x, c, ctx, c_ctx, w_ada, b_ada, norm1_g, norm2_g, w_in, ssd_conv_w, ssd_conv_b, ssd_dt_bias,
           ssd_A_log, ssd_D, ssd_norm_g, ssd_w_out, sc_conv_w, sc_w_out, w_o, peer_w_q, peer_keys,
           peer_u, peer_v, final_g):
    batch, seq_len, d = x.shape
    ctx_len = ctx.shape[1]
    depth = w_in.shape[0]
    assert depth == 1, "single-layer block"
    heads = ssd_D.shape[-1]
    d_ssd = ssd_norm_g.shape[-1]
    n_xbc = ssd_conv_w.shape[-1]
    gn = (n_xbc - d_ssd) // 2
    groups = gn // SSD_STATE
    d_sc = sc_conv_w.shape[-1]
    assert d_ssd // heads == LANE // 2 and heads % (2 * groups) == 0 and 2 * heads <= LANE
    assert d_sc == d and n_xbc % d == 0 and d_ssd % d == 0 and d % LANE == 0
    assert seq_len % SSD_CHUNK == 0 and ctx_len % SSD_CHUNK == 0

    off_dt = n_xbc
    off_z = off_dt + 2 * heads
    off_sc = off_z + d_ssd
    off_gate = off_sc + 3 * d_sc
    w_in0 = w_in[0]
    w_main = jnp.concatenate([w_in0[:, off_z:off_sc], w_in0[:, :off_dt], w_in0[:, off_sc:]],
                             axis=1).astype(BF16)
    w_dt = jnp.pad(w_in0[:, off_dt:off_z], ((0, 0), (0, LANE - 2 * heads))).astype(BF16)
    dt_bias = jnp.pad(ssd_dt_bias[0].reshape(1, 2 * heads), ((0, 0), (0, LANE - 2 * heads)))
    a_log = jnp.pad(ssd_A_log[0].reshape(1, 2 * heads).astype(F32), ((0, 0), (0, LANE - 2 * heads)))
    xbc_blk = d_ssd // d
    scb_blk = (d_ssd + n_xbc) // d
    gate_blk = scb_blk + 3
    assert off_gate - off_sc == 3 * d

    rows = -(-(batch + 1) // 8) * 8
    cvec = jnp.zeros((rows, d), F32).at[:batch].set(c).at[batch].set(c_ctx)
    mod = _ada(cvec, w_ada[0], b_ada[0])
    chunk = lambda r, i: mod[r, i * d:(i + 1) * d]
    lat = lambda i: mod[:batch, i * d:(i + 1) * d].reshape(batch, 1, d)
    sh1, sc1, g1, sh2, sc2, g2 = (lat(i) for i in range(6))
    sh_c = jnp.broadcast_to(chunk(batch, 0), (batch, 1, d))
    sc_c = jnp.broadcast_to(chunk(batch, 1), (batch, 1, d))

    conv_w = ssd_conv_w[0]
    conv_b = ssd_conv_b[0]
    p_c, dt_c = _inproj(ctx.reshape(batch * ctx_len, d), norm1_g[0], sh_c, sc_c,
                        w_main[:, d_ssd:d_ssd + n_xbc], w_dt, ctx_len)
    x3_c = _ssd_conv(p_c, conv_w, conv_b, batch, ctx_len, n_xbc, d, 0)
    dt_c = dt_c.reshape(batch, ctx_len, LANE)
    zero_state = jnp.zeros((batch, heads // 2, LANE, SSD_STATE), F32)
    _, hf_c = _ssd_scan(x3_c, dt_c, dt_bias, a_log, zero_state, heads, groups, False)
    _, hb_c = _ssd_scan(x3_c, dt_c, dt_bias, a_log, zero_state, heads, groups, True)

    x2d = x.reshape(batch * seq_len, d)
    p, dt_l = _inproj(x2d, norm1_g[0], sh1, sc1, w_main, w_dt, seq_len)
    x3 = _ssd_conv(p, conv_w, conv_b, batch, seq_len, n_xbc, d, xbc_blk)
    dt_l = dt_l.reshape(batch, seq_len, LANE)
    drow = jnp.repeat(ssd_D[0], LANE // 2).reshape(heads // 2, LANE)
    y_f, _ = _ssd_scan(x3, dt_l, dt_bias, a_log, hf_c, heads, groups, False)
    y, _ = _ssd_scan(x3, dt_l, dt_bias, a_log, hb_c, heads, groups, True, drow=drow, yprev=y_f)

    m1 = _mixa(y.reshape(batch * seq_len, d_ssd), p, ssd_norm_g[0], ssd_w_out[0].astype(BF16), gate_blk)
    x1, hq = _mixb(p, m1, x2d, g1, sc_conv_w[0], sc_w_out[0].astype(BF16), w_o[0].astype(BF16),
                   norm2_g[0], sh2, sc2, seq_len, scb_blk, gate_blk + 1)

    thr, coef, s2, e2 = _peera(hq, peer_w_q[0].T.astype(BF16), peer_keys[0].astype(BF16))
    out = _peerb(hq, thr, coef, s2, e2, peer_u[0].astype(BF16), peer_v[0].T.astype(BF16), x1, g2,
                 final_g, seq_len)
    return out.reshape(batch, seq_len, d)
```

```python
import functools
import math

import jax
import jax.numpy as jnp
from jax import lax
from jax.experimental import pallas as pl
from jax.experimental.pallas import tpu as pltpu

F32 = jnp.float32
BF16 = jnp.bfloat16

LANE = 128
BF16_ROWS = 16
MXU_WIDTH = 256
VMEM_LIMIT_BYTES = 56 * 1024 * 1024

EPS = 1e-6
SSD_CHUNK = 128
SSD_STATE = 128
GRID_W = 64
PEER_TOPK = 16
NEG = -1e30

HIGHEST = lax.Precision.HIGHEST


def _cparams(n_axes, flags=None):
    return pltpu.CompilerParams(dimension_semantics=("arbitrary",) * n_axes,
                                vmem_limit_bytes=VMEM_LIMIT_BYTES, flags=flags)


def _resident(block_shape, index_map):
    return pl.BlockSpec(block_shape, index_map, pipeline_mode=pl.Buffered(1))


def _silu(v):
    return v * jax.nn.sigmoid(v)


def _rms(v, g):
    return v * lax.rsqrt(jnp.mean(v * v, axis=-1, keepdims=True) + EPS) * g


def _nt_dot(a, b):
    return lax.dot_general(a, b, (((1,), (1,)), ((), ())), preferred_element_type=F32)


def _ada_kernel(c_ref, w_ref, b_ref, o_ref):
    s = _silu(c_ref[...])
    o_ref[...] = jnp.dot(s, w_ref[...], preferred_element_type=F32, precision=HIGHEST) + b_ref[...]


def _ada(cvec, w, b):
    rows, d = cvec.shape
    n = w.shape[1]
    tn = 1024 if n % 1024 == 0 else d
    return pl.pallas_call(
        _ada_kernel,
        grid=(n // tn,),
        in_specs=[pl.BlockSpec((rows, d), lambda j: (0, 0)),
                  pl.BlockSpec((d, tn), lambda j: (0, j)),
                  pl.BlockSpec((1, tn), lambda j: (0, j))],
        out_specs=pl.BlockSpec((rows, tn), lambda j: (0, j)),
        out_shape=jax.ShapeDtypeStruct((rows, n), F32),
        compiler_params=_cparams(1),
        name="ada",
    )(cvec, w, b.reshape(1, n))


def _inproj_kernel(x_ref, g_ref, sh_ref, sc_ref, w_ref, wdt_ref, p_ref, dt_ref, h_scr):
    @pl.when(pl.program_id(1) == 0)
    def _():
        h = _rms(x_ref[...], g_ref[...]) * (1.0 + sc_ref[0]) + sh_ref[0]
        hb = h.astype(BF16)
        h_scr[...] = hb
        dt_ref[...] = jnp.dot(hb, wdt_ref[...], preferred_element_type=F32)

    p_ref[...] = jnp.dot(h_scr[...], w_ref[...], preferred_element_type=F32).astype(p_ref.dtype)


def _inproj(x2d, norm_g, shift, scale, w_main, w_dt, seq_len):
    t, d = x2d.shape
    n = w_main.shape[1]
    tm = min(1024, seq_len)
    tn = 1024 if n % 1024 == 0 else d
    per_batch = seq_len // tm
    return pl.pallas_call(
        _inproj_kernel,
        grid=(t // tm, n // tn),
        in_specs=[pl.BlockSpec((tm, d), lambda i, j: (i, 0)),
                  pl.BlockSpec((1, d), lambda i, j: (0, 0)),
                  pl.BlockSpec((1, 1, d), lambda i, j: (i // per_batch, 0, 0)),
                  pl.BlockSpec((1, 1, d), lambda i, j: (i // per_batch, 0, 0)),
                  pl.BlockSpec((d, tn), lambda i, j: (0, j)),
                  pl.BlockSpec((d, LANE), lambda i, j: (0, 0))],
        out_specs=[pl.BlockSpec((tm, tn), lambda i, j: (i, j)),
                   pl.BlockSpec((tm, LANE), lambda i, j: (i, 0))],
        out_shape=[jax.ShapeDtypeStruct((t, n), BF16),
                   jax.ShapeDtypeStruct((t, LANE), F32)],
        scratch_shapes=[pltpu.VMEM((tm, d), BF16)],
        compiler_params=_cparams(2),
        name="inproj",
    )(x2d, norm_g.reshape(1, d), shift, scale, w_main, w_dt)


def _conv_kernel(main_ref, prev_ref, next_ref, w_ref, b_ref, o_ref, e_scr, *, rows, n_rb, taps):
    rb = pl.program_id(1)
    halo = BF16_ROWS
    pad = taps // 2
    e_scr[halo:halo + rows, :] = main_ref[...].astype(F32)
    e_scr[0:halo, :] = jnp.where(rb > 0, prev_ref[...].astype(F32), 0.0)
    e_scr[halo + rows:2 * halo + rows, :] = jnp.where(rb < n_rb - 1, next_ref[...].astype(F32), 0.0)
    acc = b_ref[...] + w_ref[0:1, :] * e_scr[halo - pad:halo - pad + rows, :]
    for k in range(1, taps):
        acc = acc + w_ref[k:k + 1, :] * e_scr[halo - pad + k:halo - pad + k + rows, :]
    act = _silu(acc).astype(o_ref.dtype)
    for t in range(o_ref.shape[1]):
        o_ref[0, t] = act[:, t * LANE:(t + 1) * LANE]


def _ssd_conv(p2d, conv_w, conv_b, batch, seq_len, n_xbc, tc, col_blk):
    taps = conv_w.shape[0]
    rows = min(512, seq_len)
    n_rb = seq_len // rows
    n_col = n_xbc // tc
    halo = BF16_ROWS
    blocks_per_rb = rows // halo
    last_halo_block = batch * seq_len // halo - 1
    kern = functools.partial(_conv_kernel, rows=rows, n_rb=n_rb, taps=taps)
    return pl.pallas_call(
        kern,
        grid=(batch, n_rb, n_col),
        in_specs=[
            pl.BlockSpec((rows, tc), lambda b, r, j: (b * n_rb + r, j + col_blk)),
            pl.BlockSpec((halo, tc),
                         lambda b, r, j: (jnp.maximum((b * n_rb + r) * blocks_per_rb - 1, 0), j + col_blk)),
            pl.BlockSpec((halo, tc),
                         lambda b, r, j: (jnp.minimum((b * n_rb + r + 1) * blocks_per_rb, last_halo_block),
                                          j + col_blk)),
            pl.BlockSpec((taps, tc), lambda b, r, j: (0, j)),
            pl.BlockSpec((1, tc), lambda b, r, j: (0, j)),
        ],
        out_specs=pl.BlockSpec((1, tc // LANE, rows, LANE), lambda b, r, j: (b, j, r, 0)),
        out_shape=jax.ShapeDtypeStruct((batch, n_xbc // LANE, seq_len, LANE), BF16),
        scratch_shapes=[pltpu.VMEM((rows + 2 * halo, tc), F32)],
        compiler_params=_cparams(3),
        name="ssdconv",
    )(p2d, p2d, p2d, conv_w, conv_b.reshape(1, n_xbc))


def _ssd_kernel(*refs, reverse, add_prev, n_chunks, heads, groups):
    if add_prev:
        (x3_ref, dtr_ref, bias_ref, alog_ref, h0_ref, drow_ref, yprev_ref,
         y_ref, hfin_ref, h_scr) = refs
    else:
        x3_ref, dtr_ref, bias_ref, alog_ref, h0_ref, y_ref, hfin_ref, h_scr = refs
    q = SSD_CHUNK
    n_pairs = heads // 2
    pairs_per_group = n_pairs // groups
    col0 = heads if reverse else 0
    step = pl.program_id(1)

    @pl.when(step == 0)
    def _():
        h_scr[...] = h0_ref[0]

    pre = dtr_ref[0] + bias_ref[...]
    dt = jnp.maximum(pre, 0.0) + jnp.log1p(jnp.exp(-jnp.abs(pre)))
    a = dt * (-jnp.exp(alog_ref[...]))
    ri = lax.broadcasted_iota(jnp.int32, (q, q), 0)
    ci = lax.broadcasted_iota(jnp.int32, (q, q), 1)
    tri = (ci >= ri) if reverse else (ci <= ri)
    tri_t = (ri >= ci) if reverse else (ri <= ci)
    acum = jnp.dot(tri.astype(F32), a, preferred_element_type=F32, precision=HIGHEST)
    a_t = a.T
    dt_t = dt.T
    acum_t = jnp.dot(a_t, tri_t.astype(F32), preferred_element_type=F32, precision=HIGHEST)
    tot = jnp.dot(a_t, jnp.ones((q, q), F32), preferred_element_type=F32, precision=HIGHEST)
    w_t = dt_t * jnp.exp(tot - acum_t)
    dec_t = jnp.exp(tot)
    low_lanes = ci < (LANE // 2)
    low_rows = ri < (LANE // 2)

    cb = None
    for kp in range(n_pairs):
        g = kp // pairs_per_group
        b_g = x3_ref[0, n_pairs + g]
        c_g = x3_ref[0, n_pairs + groups + g]
        if kp % pairs_per_group == 0:
            cb = _nt_dot(c_g, b_g)
            c_f = c_g.astype(F32)
        ms, cds = [], []
        for hc in (col0 + 2 * kp, col0 + 2 * kp + 1):
            col = jnp.broadcast_to(acum[:, hc:hc + 1], (q, q))
            seg = col - acum_t[hc:hc + 1, :]
            decay = jnp.exp(jnp.where(tri, seg, NEG))
            ms.append((cb * decay * dt_t[hc:hc + 1, :]).astype(BF16))
            cds.append((c_f * jnp.exp(col)).astype(BF16))
        xp = x3_ref[0, kp].astype(F32)
        rhs = jnp.concatenate([jnp.where(low_lanes, xp, 0.0), jnp.where(low_lanes, 0.0, xp)],
                              axis=0).astype(BF16)
        y = jnp.dot(jnp.concatenate(ms, axis=1), rhs, preferred_element_type=F32)
        hp = h_scr[kp]
        rhs2 = jnp.concatenate([jnp.where(low_rows, hp, 0.0), jnp.where(low_rows, 0.0, hp)],
                               axis=1).astype(BF16)
        y = y + _nt_dot(jnp.concatenate(cds, axis=1), rhs2)
        if add_prev:
            y = y + drow_ref[kp:kp + 1, :] * xp + yprev_ref[0, :, kp * LANE:(kp + 1) * LANE].astype(F32)
        y_ref[0, :, kp * LANE:(kp + 1) * LANE] = y.astype(y_ref.dtype)
        h1, h2 = col0 + 2 * kp, col0 + 2 * kp + 1
        w_sel = jnp.where(low_rows, w_t[h1:h1 + 1, :], w_t[h2:h2 + 1, :])
        d_sel = jnp.where(low_rows, dec_t[h1:h1 + 1, :], dec_t[h2:h2 + 1, :])
        xw_t = (xp.T * w_sel).astype(BF16)
        h_scr[kp] = d_sel * hp + jnp.dot(xw_t, b_g, preferred_element_type=F32)

    @pl.when(step == n_chunks - 1)
    def _():
        hfin_ref[0] = h_scr[...]


def _ssd_scan(x3, dtraw, dt_bias, a_log, h0, heads, groups, reverse, drow=None, yprev=None):
    batch, n_tiles, seq_len, _ = x3.shape
    n_chunks = seq_len // SSD_CHUNK
    n_pairs = heads // 2
    d_ssd = n_pairs * LANE
    add_prev = yprev is not None
    if reverse:
        chunk = lambda k: n_chunks - 1 - k
    else:
        chunk = lambda k: k
    in_specs = [
        pl.BlockSpec((1, n_tiles, SSD_CHUNK, LANE), lambda b, k: (b, 0, chunk(k), 0)),
        pl.BlockSpec((1, SSD_CHUNK, LANE), lambda b, k: (b, chunk(k), 0)),
        pl.BlockSpec((1, LANE), lambda b, k: (0, 0)),
        pl.BlockSpec((1, LANE), lambda b, k: (0, 0)),
        pl.BlockSpec((1, n_pairs, LANE, SSD_STATE), lambda b, k: (b, 0, 0, 0)),
    ]
    args = [x3, dtraw, dt_bias, a_log, h0]
    if add_prev:
        in_specs += [pl.BlockSpec((n_pairs, LANE), lambda b, k: (0, 0)),
                     pl.BlockSpec((1, SSD_CHUNK, d_ssd), lambda b, k: (b, chunk(k), 0))]
        args += [drow, yprev]
    kern = functools.partial(_ssd_kernel, reverse=reverse, add_prev=add_prev, n_chunks=n_chunks,
                             heads=heads, groups=groups)
    return pl.pallas_call(
        kern,
        grid=(batch, n_chunks),
        in_specs=in_specs,
        out_specs=[pl.BlockSpec((1, SSD_CHUNK, d_ssd), lambda b, k: (b, chunk(k), 0)),
                   pl.BlockSpec((1, n_pairs, LANE, SSD_STATE), lambda b, k: (b, 0, 0, 0))],
        out_shape=[jax.ShapeDtypeStruct((batch, seq_len, d_ssd), BF16),
                   jax.ShapeDtypeStruct((batch, n_pairs, LANE, SSD_STATE), F32)],
        scratch_shapes=[pltpu.VMEM((n_pairs, LANE, SSD_STATE), F32)],
        compiler_params=_cparams(2),
        name="ssd_bwd" if reverse else "ssd_fwd",
    )(*args)


def _mixa_kernel(y_ref, z_ref, gate_ref, ng_ref, w_ref, o_ref):
    z = z_ref[...].astype(F32)
    v = _rms(y_ref[...].astype(F32) * _silu(z), ng_ref[...])
    o = jnp.dot(v.astype(BF16), w_ref[...], preferred_element_type=F32)
    o_ref[...] = (jax.nn.sigmoid(gate_ref[...].astype(F32)) * o).astype(o_ref.dtype)


def _mixa(y2d, p2d, norm_g, w_out, gate_blk):
    t, d_ssd = y2d.shape
    d = w_out.shape[1]
    tm = min(256, t)
    return pl.pallas_call(
        _mixa_kernel,
        grid=(t // tm,),
        in_specs=[pl.BlockSpec((tm, d_ssd), lambda i: (i, 0)),
                  pl.BlockSpec((tm, d_ssd), lambda i: (i, 0)),
                  pl.BlockSpec((tm, d), lambda i: (i, gate_blk)),
                  pl.BlockSpec((1, d_ssd), lambda i: (0, 0)),
                  _resident((d_ssd, d), lambda i: (0, 0))],
        out_specs=pl.BlockSpec((tm, d), lambda i: (i, 0)),
        out_shape=jax.ShapeDtypeStruct((t, d), BF16),
        compiler_params=_cparams(1),
        name="mixa",
    )(y2d, p2d, p2d, norm_g.reshape(1, d_ssd), w_out)


def _mixb_kernel(scb_ref, scc_ref, scx_ref, gsc_ref, m1_ref, x_ref, g1_ref, cw_ref, wsc_ref, wo_ref,
                 n2_ref, sh2_ref, sc2_ref, x1_ref, hq_ref, u_scr, *, rows):
    pad = 8
    u_scr[0:pad, :] = jnp.zeros((pad, u_scr.shape[1]), F32)
    u_scr[pad + rows:2 * pad + rows, :] = jnp.zeros((pad, u_scr.shape[1]), F32)
    u_scr[pad:pad + rows, :] = scc_ref[...].astype(F32) * scx_ref[...].astype(F32)
    pos = jnp.bitwise_and(lax.broadcasted_iota(jnp.int32, (rows, u_scr.shape[1]), 0), GRID_W - 1)
    left = jnp.where(pos == 0, 0.0, u_scr[pad - 1:pad - 1 + rows, :])
    right = jnp.where(pos == GRID_W - 1, 0.0, u_scr[pad + 1:pad + 1 + rows, :])
    conv = cw_ref[0:1, :] * left + cw_ref[1:2, :] * u_scr[pad:pad + rows, :] + cw_ref[2:3, :] * right
    v = (scb_ref[...].astype(F32) * conv).astype(BF16)
    y_sc = jnp.dot(v, wsc_ref[...], preferred_element_type=F32)
    merged = m1_ref[...].astype(F32) + jax.nn.sigmoid(gsc_ref[...].astype(F32)) * y_sc
    out = jnp.dot(merged.astype(BF16), wo_ref[...], preferred_element_type=F32)
    x1 = x_ref[...] + g1_ref[0] * out
    x1_ref[...] = x1
    hq_ref[...] = (_rms(x1, n2_ref[...]) * (1.0 + sc2_ref[0]) + sh2_ref[0]).astype(hq_ref.dtype)


def _mixb(p2d, m1, x2d, g1, sc_conv_w, w_sc, w_o, norm2_g, sh2, sc2, seq_len, scb_blk, gsc_blk):
    t, d = x2d.shape
    tm = min(256, t)
    per_batch = seq_len // tm
    bmap = lambda i: (i // per_batch, 0, 0)
    tile = lambda blk: pl.BlockSpec((tm, d), lambda i: (i, blk))
    kern = functools.partial(_mixb_kernel, rows=tm)
    return pl.pallas_call(
        kern,
        grid=(t // tm,),
        in_specs=[tile(scb_blk), tile(scb_blk + 1), tile(scb_blk + 2), tile(gsc_blk),
                  tile(0), tile(0),
                  pl.BlockSpec((1, 1, d), bmap),
                  pl.BlockSpec(sc_conv_w.shape, lambda i: (0, 0)),
                  _resident((d, d), lambda i: (0, 0)),
                  _resident((d, d), lambda i: (0, 0)),
                  pl.BlockSpec((1, d), lambda i: (0, 0)),
                  pl.BlockSpec((1, 1, d), bmap),
                  pl.BlockSpec((1, 1, d), bmap)],
        out_specs=[tile(0), tile(0)],
        out_shape=[jax.ShapeDtypeStruct((t, d), F32), jax.ShapeDtypeStruct((t, d), BF16)],
        scratch_shapes=[pltpu.VMEM((tm + 16, d), F32)],
        compiler_params=_cparams(1),
        name="mixb",
    )(p2d, p2d, p2d, p2d, m1, x2d, g1, sc_conv_w, w_sc, w_o, norm2_g.reshape(1, d), sh2, sc2)


NO_RANK = 127.0


def _top_values(cur, n, sv_ref, want_rank=False):
    rank = jnp.full(cur.shape, NO_RANK, F32) if want_rank else None
    for it in range(n):
        m = jnp.max(cur, axis=0, keepdims=True)
        sv_ref[it:it + 1, :] = m
        hit = cur >= m
        if want_rank:
            rank = jnp.where(hit, float(it), rank)
        cur = jnp.where(hit, NEG, cur)
    return rank


def _peera_kernel(hq_ref, wq_ref, keys_ref, cnt_ref, coef_ref, rank_ref, e2_ref,
                  sv1_scr, sv2_scr, cand_scr, svc_scr):
    k = PEER_TOPK
    n_heads = keys_ref.shape[0]
    dk = keys_ref.shape[3]
    q_t = _nt_dot(wq_ref[...], hq_ref[...])
    for h in range(n_heads):
        s_t = []
        for s in range(2):
            r0 = (2 * h + s) * dk
            s_t.append(jnp.dot(keys_ref[h, s], q_t[r0:r0 + dk, :].astype(BF16),
                               preferred_element_type=F32))
        _top_values(s_t[0], k + 1, sv1_scr)
        rank2 = _top_values(s_t[1], k + 1, sv2_scr, want_rank=True)
        sv2 = sv2_scr[0:k, :]
        for a in range(k):
            cand_scr[a * k:(a + 1) * k, :] = sv1_scr[a:a + 1, :] + sv2
        cand = cand_scr[...]
        _top_values(cand, k + 1, svc_scr)
        c_next = jnp.maximum(svc_scr[k:k + 1, :],
                             jnp.maximum(sv1_scr[k:k + 1, :] + sv2_scr[0:1, :],
                                         sv1_scr[0:1, :] + sv2_scr[k:k + 1, :]))
        tau = 0.5 * (svc_scr[k - 1:k, :] + c_next)
        top = sv1_scr[0:1, :] + sv2_scr[0:1, :]
        z = jnp.sum(jnp.where(cand > tau, jnp.exp(cand - top), 0.0), axis=0, keepdims=True)
        thr = tau - s_t[0]
        count = jnp.zeros_like(thr)
        for b in range(k):
            count = count + jnp.where(sv2_scr[b:b + 1, :] >= thr, 1.0, 0.0)
        cnt_ref[h] = count
        coef_ref[h] = jnp.exp(s_t[0] - sv1_scr[0:1, :]) / z
        rank_ref[h] = rank2.astype(BF16)
        e2_ref[h] = jnp.exp(s_t[1] - sv2_scr[0:1, :]).astype(BF16)


def _peera(hq, wq_t, keys):
    t, d = hq.shape
    n_heads, _, n_keys, _ = keys.shape
    tt = min(512, t)
    k = PEER_TOPK
    shape = (n_heads, n_keys, t)
    ospec = pl.BlockSpec((n_heads, n_keys, tt), lambda i: (0, 0, i))
    return pl.pallas_call(
        _peera_kernel,
        grid=(t // tt,),
        in_specs=[pl.BlockSpec((tt, d), lambda i: (i, 0)),
                  _resident(wq_t.shape, lambda i: (0, 0)),
                  pl.BlockSpec(keys.shape, lambda i: (0, 0, 0, 0))],
        out_specs=[ospec] * 4,
        out_shape=[jax.ShapeDtypeStruct(shape, F32), jax.ShapeDtypeStruct(shape, F32),
                   jax.ShapeDtypeStruct(shape, BF16), jax.ShapeDtypeStruct(shape, BF16)],
        scratch_shapes=[pltpu.VMEM((k + 8, tt), F32), pltpu.VMEM((k + 8, tt), F32),
                        pltpu.VMEM((k * k, tt), F32), pltpu.VMEM((k + 8, tt), F32)],
        compiler_params=_cparams(1),
        name="peera",
    )(hq, wq_t, keys)


def _peerb_kernel(hq_ref, cnt_ref, coef_ref, rank_ref, e2_ref, u_ref, vt_ref, x1_ref, g2_ref, fg_ref,
                  o_ref, acc_scr, a0_scr, a1_scr, wg_new, row_scr, *, n_eb):
    e = pl.program_id(1)
    n_heads, n_keys, _ = rank_ref.shape

    @pl.when((pl.program_id(0) == 0) & (e == 0))
    def _():
        a1_scr[...] = jnp.zeros_like(a1_scr)

    @pl.when(e == 0)
    def _():
        acc_scr[...] = jnp.zeros_like(acc_scr)

    def stages(a_new, a_old):
        half = jnp.where(e >= 1, 0.5, 0.0)
        n_i = cnt_ref.shape[1]
        for h in range(n_heads):
            for ii in range(n_i):
                for tab, ref in enumerate((cnt_ref, coef_ref)):
                    row = jnp.broadcast_to(ref[h, ii:ii + 1, :], (BF16_ROWS, ref.shape[2]))
                    row_scr[tab, h * n_i + ii] = row.astype(BF16)

        def gated_weights(sub):
            lanes = slice(sub * LANE, (sub + 1) * LANE)
            n_jc = n_keys // BF16_ROWS
            for ii in range(n_i):
                w = [None] * n_jc
                for h in range(n_heads):
                    cnt = row_scr[0, h * n_i + ii, :, lanes]
                    coef = row_scr[1, h * n_i + ii, :, lanes]
                    for jc in range(n_jc):
                        jrows = slice(jc * BF16_ROWS, (jc + 1) * BF16_ROWS)
                        sel = jnp.clip(cnt - rank_ref[h, jrows, lanes], 0.0, 1.0)
                        term = (sel * e2_ref[h, jrows, lanes]) * coef
                        w[jc] = term if w[jc] is None else w[jc] + term
                for jc in range(n_jc):
                    rows = slice(ii * n_keys + jc * BF16_ROWS, ii * n_keys + (jc + 1) * BF16_ROWS)
                    act = a_old[rows, lanes]
                    gelu = (half * act) * (1.0 + lax.erf(act * (1.0 / math.sqrt(2.0))))
                    wg_new[rows, lanes] = w[jc] * gelu.astype(BF16)

        tt = rank_ref.shape[2]
        piece = min(MXU_WIDTH, tt)
        subs = piece // LANE
        for pc in range(tt // piece):
            cols = slice(pc * piece, (pc + 1) * piece)
            for sub in range(pc * subs, pc * subs + (subs + 1) // 2):
                gated_weights(sub)
            a_new[:, cols] = _nt_dot(u_ref[...], hq_ref[cols, :])
            for sub in range(pc * subs + (subs + 1) // 2, (pc + 1) * subs):
                gated_weights(sub)
            acc_scr[:, cols] += jnp.dot(vt_ref[...], wg_new[:, cols], preferred_element_type=F32)

    @pl.when(e % 2 == 0)
    def _():
        stages(a0_scr, a1_scr)

    @pl.when(e % 2 == 1)
    def _():
        stages(a1_scr, a0_scr)

    @pl.when(e == n_eb)
    def _():
        x2 = x1_ref[...] + g2_ref[0] * acc_scr[...].T
        o_ref[...] = _rms(x2, fg_ref[...])


def _peerb(hq, cnt, coef, rank, e2, u, v_t, x1, g2, final_g, seq_len):
    t, d = hq.shape
    n_heads, n_keys, _ = rank.shape
    n_exp = u.shape[0]
    tt = min(512, t)
    i_blk = 8
    et = i_blk * n_keys
    per_batch = seq_len // tt
    n_eb = n_exp // et
    blk = lambda e, lag: jnp.clip(e - lag, 0, n_eb - 1)
    stat = pl.BlockSpec((n_heads, n_keys, tt), lambda i, e: (0, 0, i), pipeline_mode=pl.Buffered(1))
    rowstat = pl.BlockSpec((n_heads, i_blk, tt), lambda i, e: (0, blk(e, 1), i))
    kern = functools.partial(_peerb_kernel, n_eb=n_eb)
    return pl.pallas_call(
        kern,
        grid=(t // tt, n_eb + 1),
        in_specs=[pl.BlockSpec((tt, d), lambda i, e: (i, 0)),
                  rowstat, rowstat, stat, stat,
                  pl.BlockSpec((et, d), lambda i, e: (blk(e, 0), 0)),
                  pl.BlockSpec((d, et), lambda i, e: (0, blk(e, 1))),
                  pl.BlockSpec((tt, d), lambda i, e: (i, 0), pipeline_mode=pl.Buffered(1)),
                  pl.BlockSpec((1, 1, d), lambda i, e: (i // per_batch, 0, 0)),
                  pl.BlockSpec((1, d), lambda i, e: (0, 0))],
        out_specs=pl.BlockSpec((tt, d), lambda i, e: (i, 0)),
        out_shape=jax.ShapeDtypeStruct((t, d), F32),
        scratch_shapes=[pltpu.VMEM((d, tt), F32),
                        pltpu.VMEM((et, tt), F32), pltpu.VMEM((et, tt), F32),
                        pltpu.VMEM((et, tt), BF16),
                        pltpu.VMEM((2, n_heads * i_blk, BF16_ROWS, tt), BF16)],
        compiler_params=_cparams(2),
        name="peerb",
    )(hq, cnt, coef, rank, e2, u, v_t, x1, g2, final_g.reshape(1, d))


def kernel(x, c, ctx, c_ctx, w_ada, b_ada, norm1_g, norm2_g, w_in, ssd_conv_w, ssd_conv_b, ssd_dt_bias,
           ssd_A_log, ssd_D, ssd_norm_g, ssd_w_out, sc_conv_w, sc_w_out, w_o, peer_w_q, peer_keys,
           peer_u, peer_v, final_g):
    batch, seq_len, d = x.shape
    ctx_len = ctx.shape[1]
    depth = w_in.shape[0]
    assert depth == 1, "single-layer block"
    heads = ssd_D.shape[-1]
    d_ssd = ssd_norm_g.shape[-1]
    n_xbc = ssd_conv_w.shape[-1]
    gn = (n_xbc - d_ssd) // 2
    groups = gn // SSD_STATE
    d_sc = sc_conv_w.shape[-1]
    assert d_ssd // heads == LANE // 2 and heads % (2 * groups) == 0 and 2 * heads <= LANE
    assert d_sc == d and n_xbc % d == 0 and d_ssd % d == 0 and d % LANE == 0
    assert seq_len % SSD_CHUNK == 0 and ctx_len % SSD_CHUNK == 0

    off_dt = n_xbc
    off_z = off_dt + 2 * heads
    off_sc = off_z + d_ssd
    off_gate = off_sc + 3 * d_sc
    w_in0 = w_in[0]
    w_main = jnp.concatenate([w_in0[:, off_z:off_sc], w_in0[:, :off_dt], w_in0[:, off_sc:]],
                             axis=1).astype(BF16)
    w_dt = jnp.pad(w_in0[:, off_dt:off_z], ((0, 0), (0, LANE - 2 * heads))).astype(BF16)
    dt_bias = jnp.pad(ssd_dt_bias[0].reshape(1, 2 * heads), ((0, 0), (0, LANE - 2 * heads)))
    a_log = jnp.pad(ssd_A_log[0].reshape(1, 2 * heads).astype(F32), ((0, 0), (0, LANE - 2 * heads)))
    xbc_blk = d_ssd // d
    scb_blk = (d_ssd + n_xbc) // d
    gate_blk = scb_blk + 3
    assert off_gate - off_sc == 3 * d

    rows = -(-(batch + 1) // 8) * 8
    cvec = jnp.zeros((rows, d), F32).at[:batch].set(c).at[batch].set(c_ctx)
    mod = _ada(cvec, w_ada[0], b_ada[0])
    chunk = lambda r, i: mod[r, i * d:(i + 1) * d]
    lat = lambda i: mod[:batch, i * d:(i + 1) * d].reshape(batch, 1, d)
    sh1, sc1, g1, sh2, sc2, g2 = (lat(i) for i in range(6))
    sh_c = jnp.broadcast_to(chunk(batch, 0), (batch, 1, d))
    sc_c = jnp.broadcast_to(chunk(batch, 1), (batch, 1, d))

    conv_w = ssd_conv_w[0]
    conv_b = ssd_conv_b[0]
    p_c, dt_c = _inproj(ctx.reshape(batch * ctx_len, d), norm1_g[0], sh_c, sc_c,
                        w_main[:, d_ssd:d_ssd + n_xbc], w_dt, ctx_len)
    x3_c = _ssd_conv(p_c, conv_w, conv_b, batch, ctx_len, n_xbc, d, 0)
    dt_c = dt_c.reshape(batch, ctx_len, LANE)
    zero_state = jnp.zeros((batch, heads // 2, LANE, SSD_STATE), F32)
    _, hf_c = _ssd_scan(x3_c, dt_c, dt_bias, a_log, zero_state, heads, groups, False)
    _, hb_c = _ssd_scan(x3_c, dt_c, dt_bias, a_log, zero_state, heads, groups, True)

    x2d = x.reshape(batch * seq_len, d)
    p, dt_l = _inproj(x2d, norm1_g[0], sh1, sc1, w_main, w_dt, seq_len)
    x3 = _ssd_conv(p, conv_w, conv_b, batch, seq_len, n_xbc, d, xbc_blk)
    dt_l = dt_l.reshape(batch, seq_len, LANE)
    drow = jnp.repeat(ssd_D[0], LANE // 2).reshape(heads // 2, LANE)
    y_f, _ = _ssd_scan(x3, dt_l, dt_bias, a_log, hf_c, heads, groups, False)
    y, _ = _ssd_scan(x3, dt_l, dt_bias, a_log, hb_c, heads, groups, True, drow=drow, yprev=y_f)

    m1 = _mixa(y.reshape(batch * seq_len, d_ssd), p, ssd_norm_g[0], ssd_w_out[0].astype(BF16), gate_blk)
    x1, hq = _mixb(p, m1, x2d, g1, sc_conv_w[0], sc_w_out[0].astype(BF16), w_o[0].astype(BF16),
                   norm2_g[0], sh2, sc2, seq_len, scb_blk, gate_blk + 1)

    cnt, coef, rank, e2 = _peera(hq, peer_w_q[0].T.astype(BF16), peer_keys[0].astype(BF16))
    out = _peerb(hq, cnt, coef, rank, e2, peer_u[0].astype(BF16), peer_v[0].T.astype(BF16), x1, g2,
                 final_g, seq_len)
    return out.reshape(batch, seq_len, d)
```

```python
import functools
import math

import jax
import jax.numpy as jnp
from jax import lax
from jax.experimental import pallas as pl
from jax.experimental.pallas import tpu as pltpu

F32 = jnp.float32
BF16 = jnp.bfloat16

LANE = 128
BF16_ROWS = 16
MXU_WIDTH = 256
VMEM_LIMIT_BYTES = 56 * 1024 * 1024

EPS = 1e-6
SSD_CHUNK = 128
SSD_STATE = 128
GRID_W = 64
PEER_TOPK = 16
NEG = -1e30

HIGHEST = lax.Precision.HIGHEST


def _cparams(n_axes, flags=None):
    return pltpu.CompilerParams(dimension_semantics=("arbitrary",) * n_axes,
                                vmem_limit_bytes=VMEM_LIMIT_BYTES, flags=flags)


def _resident(block_shape, index_map):
    return pl.BlockSpec(block_shape, index_map, pipeline_mode=pl.Buffered(1))


def _silu(v):
    return v * jax.nn.sigmoid(v)


def _rms(v, g):
    return v * lax.rsqrt(jnp.mean(v * v, axis=-1, keepdims=True) + EPS) * g


def _nt_dot(a, b):
    return lax.dot_general(a, b, (((1,), (1,)), ((), ())), preferred_element_type=F32)


def _ada_kernel(c_ref, w_ref, b_ref, o_ref):
    s = _silu(c_ref[...])
    o_ref[...] = jnp.dot(s, w_ref[...], preferred_element_type=F32, precision=HIGHEST) + b_ref[...]


def _ada(cvec, w, b):
    rows, d = cvec.shape
    n = w.shape[1]
    tn = 1024 if n % 1024 == 0 else d
    return pl.pallas_call(
        _ada_kernel,
        grid=(n // tn,),
        in_specs=[pl.BlockSpec((rows, d), lambda j: (0, 0)),
                  pl.BlockSpec((d, tn), lambda j: (0, j)),
                  pl.BlockSpec((1, tn), lambda j: (0, j))],
        out_specs=pl.BlockSpec((rows, tn), lambda j: (0, j)),
        out_shape=jax.ShapeDtypeStruct((rows, n), F32),
        compiler_params=_cparams(1),
        name="ada",
    )(cvec, w, b.reshape(1, n))


def _inproj_kernel(x_ref, g_ref, sh_ref, sc_ref, w_ref, wdt_ref, p_ref, dt_ref, h_scr):
    @pl.when(pl.program_id(1) == 0)
    def _():
        h = _rms(x_ref[...], g_ref[...]) * (1.0 + sc_ref[0]) + sh_ref[0]
        hb = h.astype(BF16)
        h_scr[...] = hb
        dt_ref[...] = jnp.dot(hb, wdt_ref[...], preferred_element_type=F32)

    p_ref[...] = jnp.dot(h_scr[...], w_ref[...], preferred_element_type=F32).astype(p_ref.dtype)


def _inproj(x2d, norm_g, shift, scale, w_main, w_dt, seq_len):
    t, d = x2d.shape
    n_blk, _, tn = w_main.shape
    tm = min(512, seq_len)
    per_batch = seq_len // tm
    return pl.pallas_call(
        _inproj_kernel,
        grid=(t // tm, n_blk),
        in_specs=[pl.BlockSpec((tm, d), lambda i, j: (i, 0)),
                  pl.BlockSpec((1, d), lambda i, j: (0, 0)),
                  pl.BlockSpec((1, 1, d), lambda i, j: (i // per_batch, 0, 0)),
                  pl.BlockSpec((1, 1, d), lambda i, j: (i // per_batch, 0, 0)),
                  pl.BlockSpec((None, d, tn), lambda i, j: (j, 0, 0)),
                  pl.BlockSpec((d, LANE), lambda i, j: (0, 0))],
        out_specs=[pl.BlockSpec((None, tm, tn), lambda i, j: (j, i, 0)),
                   pl.BlockSpec((tm, LANE), lambda i, j: (i, 0))],
        out_shape=[jax.ShapeDtypeStruct((n_blk, t, tn), BF16),
                   jax.ShapeDtypeStruct((t, LANE), F32)],
        scratch_shapes=[pltpu.VMEM((tm, d), BF16)],
        compiler_params=_cparams(2),
        name="inproj",
    )(x2d, norm_g.reshape(1, d), shift, scale, w_main, w_dt)


def _conv_kernel(main_ref, prev_ref, next_ref, w_ref, b_ref, o_ref, e_scr, *, rows, n_rb, taps):
    rb = pl.program_id(1)
    halo = BF16_ROWS
    pad = taps // 2
    e_scr[halo:halo + rows, :] = main_ref[...].astype(F32)
    e_scr[0:halo, :] = jnp.where(rb > 0, prev_ref[...].astype(F32), 0.0)
    e_scr[halo + rows:2 * halo + rows, :] = jnp.where(rb < n_rb - 1, next_ref[...].astype(F32), 0.0)
    acc = b_ref[...] + w_ref[0:1, :] * e_scr[halo - pad:halo - pad + rows, :]
    for k in range(1, taps):
        acc = acc + w_ref[k:k + 1, :] * e_scr[halo - pad + k:halo - pad + k + rows, :]
    act = _silu(acc).astype(o_ref.dtype)
    for t in range(o_ref.shape[1]):
        o_ref[0, t] = act[:, t * LANE:(t + 1) * LANE]


def _ssd_conv(p3, conv_w, conv_b, batch, seq_len, n_xbc, col_blk):
    tc = p3.shape[2]
    taps = conv_w.shape[0]
    rows = min(512, seq_len)
    n_rb = seq_len // rows
    n_col = n_xbc // tc
    halo = BF16_ROWS
    blocks_per_rb = rows // halo
    last_halo_block = batch * seq_len // halo - 1
    kern = functools.partial(_conv_kernel, rows=rows, n_rb=n_rb, taps=taps)
    return pl.pallas_call(
        kern,
        grid=(batch, n_rb, n_col),
        in_specs=[
            pl.BlockSpec((None, rows, tc), lambda b, r, j: (j + col_blk, b * n_rb + r, 0)),
            pl.BlockSpec((None, halo, tc),
                         lambda b, r, j: (j + col_blk, jnp.maximum((b * n_rb + r) * blocks_per_rb - 1, 0), 0)),
            pl.BlockSpec((None, halo, tc),
                         lambda b, r, j: (j + col_blk,
                                          jnp.minimum((b * n_rb + r + 1) * blocks_per_rb, last_halo_block), 0)),
            pl.BlockSpec((taps, tc), lambda b, r, j: (0, j)),
            pl.BlockSpec((1, tc), lambda b, r, j: (0, j)),
        ],
        out_specs=pl.BlockSpec((1, tc // LANE, rows, LANE), lambda b, r, j: (b, j, r, 0)),
        out_shape=jax.ShapeDtypeStruct((batch, n_xbc // LANE, seq_len, LANE), BF16),
        scratch_shapes=[pltpu.VMEM((rows + 2 * halo, tc), F32)],
        compiler_params=_cparams(3),
        name="ssdconv",
    )(p3, p3, p3, conv_w, conv_b.reshape(1, n_xbc))


def _ssd_kernel(*refs, reverse, add_prev, n_chunks, heads, groups):
    if add_prev:
        (x3_ref, dtr_ref, bias_ref, alog_ref, h0_ref, drow_ref, yprev_ref,
         y_ref, hfin_ref, h_scr) = refs
    else:
        x3_ref, dtr_ref, bias_ref, alog_ref, h0_ref, y_ref, hfin_ref, h_scr = refs
    q = SSD_CHUNK
    n_pairs = heads // 2
    pairs_per_group = n_pairs // groups
    col0 = heads if reverse else 0
    step = pl.program_id(1)

    @pl.when(step == 0)
    def _():
        h_scr[...] = h0_ref[0]

    pre = dtr_ref[0] + bias_ref[...]
    dt = jnp.maximum(pre, 0.0) + jnp.log1p(jnp.exp(-jnp.abs(pre)))
    a = dt * (-jnp.exp(alog_ref[...]))
    ri = lax.broadcasted_iota(jnp.int32, (q, q), 0)
    ci = lax.broadcasted_iota(jnp.int32, (q, q), 1)
    tri = (ci >= ri) if reverse else (ci <= ri)
    tri_t = (ri >= ci) if reverse else (ri <= ci)
    acum = jnp.dot(tri.astype(F32), a, preferred_element_type=F32, precision=HIGHEST)
    a_t = a.T
    dt_t = dt.T
    acum_t = jnp.dot(a_t, tri_t.astype(F32), preferred_element_type=F32, precision=HIGHEST)
    tot = jnp.dot(a_t, jnp.ones((q, q), F32), preferred_element_type=F32, precision=HIGHEST)
    w_t = dt_t * jnp.exp(tot - acum_t)
    dec_t = jnp.exp(tot)
    low_lanes = ci < (LANE // 2)
    low_rows = ri < (LANE // 2)

    cb = None
    for kp in range(n_pairs):
        g = kp // pairs_per_group
        b_g = x3_ref[0, n_pairs + g]
        c_g = x3_ref[0, n_pairs + groups + g]
        if kp % pairs_per_group == 0:
            cb = _nt_dot(c_g, b_g)
            c_f = c_g.astype(F32)
        ms, cds = [], []
        for hc in (col0 + 2 * kp, col0 + 2 * kp + 1):
            col = jnp.broadcast_to(acum[:, hc:hc + 1], (q, q))
            seg = col - acum_t[hc:hc + 1, :]
            decay = jnp.exp(jnp.where(tri, seg, NEG))
            ms.append((cb * decay * dt_t[hc:hc + 1, :]).astype(BF16))
            cds.append((c_f * jnp.exp(col)).astype(BF16))
        xp = x3_ref[0, kp].astype(F32)
        rhs = jnp.concatenate([jnp.where(low_lanes, xp, 0.0), jnp.where(low_lanes, 0.0, xp)],
                              axis=0).astype(BF16)
        y = jnp.dot(jnp.concatenate(ms, axis=1), rhs, preferred_element_type=F32)
        hp = h_scr[kp]
        rhs2 = jnp.concatenate([jnp.where(low_rows, hp, 0.0), jnp.where(low_rows, 0.0, hp)],
                               axis=1).astype(BF16)
        y = y + _nt_dot(jnp.concatenate(cds, axis=1), rhs2)
        if add_prev:
            y = y + drow_ref[kp:kp + 1, :] * xp + yprev_ref[0, :, kp * LANE:(kp + 1) * LANE].astype(F32)
        y_ref[0, :, kp * LANE:(kp + 1) * LANE] = y.astype(y_ref.dtype)
        h1, h2 = col0 + 2 * kp, col0 + 2 * kp + 1
        w_sel = jnp.where(low_rows, w_t[h1:h1 + 1, :], w_t[h2:h2 + 1, :])
        d_sel = jnp.where(low_rows, dec_t[h1:h1 + 1, :], dec_t[h2:h2 + 1, :])
        xw_t = (xp.T * w_sel).astype(BF16)
        h_scr[kp] = d_sel * hp + jnp.dot(xw_t, b_g, preferred_element_type=F32)

    @pl.when(step == n_chunks - 1)
    def _():
        hfin_ref[0] = h_scr[...]


def _ssd_scan(x3, dtraw, dt_bias, a_log, h0, heads, groups, reverse, drow=None, yprev=None):
    batch, n_tiles, seq_len, _ = x3.shape
    n_chunks = seq_len // SSD_CHUNK
    n_pairs = heads // 2
    d_ssd = n_pairs * LANE
    add_prev = yprev is not None
    if reverse:
        chunk = lambda k: n_chunks - 1 - k
    else:
        chunk = lambda k: k
    in_specs = [
        pl.BlockSpec((1, n_tiles, SSD_CHUNK, LANE), lambda b, k: (b, 0, chunk(k), 0)),
        pl.BlockSpec((1, SSD_CHUNK, LANE), lambda b, k: (b, chunk(k), 0)),
        pl.BlockSpec((1, LANE), lambda b, k: (0, 0)),
        pl.BlockSpec((1, LANE), lambda b, k: (0, 0)),
        pl.BlockSpec((1, n_pairs, LANE, SSD_STATE), lambda b, k: (b, 0, 0, 0)),
    ]
    args = [x3, dtraw, dt_bias, a_log, h0]
    if add_prev:
        in_specs += [pl.BlockSpec((n_pairs, LANE), lambda b, k: (0, 0)),
                     pl.BlockSpec((1, SSD_CHUNK, d_ssd), lambda b, k: (b, chunk(k), 0))]
        args += [drow, yprev]
    kern = functools.partial(_ssd_kernel, reverse=reverse, add_prev=add_prev, n_chunks=n_chunks,
                             heads=heads, groups=groups)
    return pl.pallas_call(
        kern,
        grid=(batch, n_chunks),
        in_specs=in_specs,
        out_specs=[pl.BlockSpec((1, SSD_CHUNK, d_ssd), lambda b, k: (b, chunk(k), 0)),
                   pl.BlockSpec((1, n_pairs, LANE, SSD_STATE), lambda b, k: (b, 0, 0, 0))],
        out_shape=[jax.ShapeDtypeStruct((batch, seq_len, d_ssd), BF16),
                   jax.ShapeDtypeStruct((batch, n_pairs, LANE, SSD_STATE), F32)],
        scratch_shapes=[pltpu.VMEM((n_pairs, LANE, SSD_STATE), F32)],
        compiler_params=_cparams(2),
        name="ssd_bwd" if reverse else "ssd_fwd",
    )(*args)


def _mixa_kernel(*refs):
    y_ref, *z_refs, gate_ref, ng_ref, w_ref, o_ref = refs
    z = jnp.concatenate([r[...] for r in z_refs], axis=1).astype(F32)
    v = _rms(y_ref[...].astype(F32) * _silu(z), ng_ref[...])
    o = jnp.dot(v.astype(BF16), w_ref[...], preferred_element_type=F32)
    o_ref[...] = (jax.nn.sigmoid(gate_ref[...].astype(F32)) * o).astype(o_ref.dtype)


def _mixa(y2d, p3, norm_g, w_out, gate_blk):
    t, d_ssd = y2d.shape
    d = w_out.shape[1]
    tm = min(256, t)
    n_z = d_ssd // d
    return pl.pallas_call(
        _mixa_kernel,
        grid=(t // tm,),
        in_specs=[pl.BlockSpec((tm, d_ssd), lambda i: (i, 0))]
                 + [pl.BlockSpec((None, tm, d), functools.partial(lambda k, i: (k, i, 0), k))
                    for k in range(n_z)]
                 + [pl.BlockSpec((None, tm, d), lambda i: (gate_blk, i, 0)),
                  pl.BlockSpec((1, d_ssd), lambda i: (0, 0)),
                  _resident((d_ssd, d), lambda i: (0, 0))],
        out_specs=pl.BlockSpec((tm, d), lambda i: (i, 0)),
        out_shape=jax.ShapeDtypeStruct((t, d), BF16),
        compiler_params=_cparams(1),
        name="mixa",
    )(y2d, *([p3] * (n_z + 1)), norm_g.reshape(1, d_ssd), w_out)


def _mixb_kernel(scb_ref, scc_ref, scx_ref, gsc_ref, m1_ref, x_ref, g1_ref, cw_ref, wsc_ref, wo_ref,
                 n2_ref, sh2_ref, sc2_ref, x1_ref, hq_ref, u_scr, *, rows):
    pad = 8
    u_scr[0:pad, :] = jnp.zeros((pad, u_scr.shape[1]), F32)
    u_scr[pad + rows:2 * pad + rows, :] = jnp.zeros((pad, u_scr.shape[1]), F32)
    u_scr[pad:pad + rows, :] = scc_ref[...].astype(F32) * scx_ref[...].astype(F32)
    pos = jnp.bitwise_and(lax.broadcasted_iota(jnp.int32, (rows, u_scr.shape[1]), 0), GRID_W - 1)
    left = jnp.where(pos == 0, 0.0, u_scr[pad - 1:pad - 1 + rows, :])
    right = jnp.where(pos == GRID_W - 1, 0.0, u_scr[pad + 1:pad + 1 + rows, :])
    conv = cw_ref[0:1, :] * left + cw_ref[1:2, :] * u_scr[pad:pad + rows, :] + cw_ref[2:3, :] * right
    v = (scb_ref[...].astype(F32) * conv).astype(BF16)
    y_sc = jnp.dot(v, wsc_ref[...], preferred_element_type=F32)
    merged = m1_ref[...].astype(F32) + jax.nn.sigmoid(gsc_ref[...].astype(F32)) * y_sc
    out = jnp.dot(merged.astype(BF16), wo_ref[...], preferred_element_type=F32)
    x1 = x_ref[...] + g1_ref[0] * out
    x1_ref[...] = x1
    hq_ref[...] = (_rms(x1, n2_ref[...]) * (1.0 + sc2_ref[0]) + sh2_ref[0]).astype(hq_ref.dtype)


def _mixb(p3, m1, x2d, g1, sc_conv_w, w_sc, w_o, norm2_g, sh2, sc2, seq_len, scb_blk, gsc_blk):
    t, d = x2d.shape
    tm = min(256, t)
    per_batch = seq_len // tm
    bmap = lambda i: (i // per_batch, 0, 0)
    tile = lambda blk: pl.BlockSpec((tm, d), lambda i: (i, blk))
    ptile = lambda blk: pl.BlockSpec((None, tm, d), lambda i: (blk, i, 0))
    kern = functools.partial(_mixb_kernel, rows=tm)
    return pl.pallas_call(
        kern,
        grid=(t // tm,),
        in_specs=[ptile(scb_blk), ptile(scb_blk + 1), ptile(scb_blk + 2), ptile(gsc_blk),
                  tile(0), tile(0),
                  pl.BlockSpec((1, 1, d), bmap),
                  pl.BlockSpec(sc_conv_w.shape, lambda i: (0, 0)),
                  _resident((d, d), lambda i: (0, 0)),
                  _resident((d, d), lambda i: (0, 0)),
                  pl.BlockSpec((1, d), lambda i: (0, 0)),
                  pl.BlockSpec((1, 1, d), bmap),
                  pl.BlockSpec((1, 1, d), bmap)],
        out_specs=[tile(0), tile(0)],
        out_shape=[jax.ShapeDtypeStruct((t, d), F32), jax.ShapeDtypeStruct((t, d), BF16)],
        scratch_shapes=[pltpu.VMEM((tm + 16, d), F32)],
        compiler_params=_cparams(1),
        name="mixb",
    )(p3, p3, p3, p3, m1, x2d, g1, sc_conv_w, w_sc, w_o, norm2_g.reshape(1, d), sh2, sc2)


NO_RANK = 127.0


def _top_values(cur, n, sv_ref, want_rank=False):
    rank = jnp.full(cur.shape, NO_RANK, F32) if want_rank else None
    for it in range(n):
        m = jnp.max(cur, axis=0, keepdims=True)
        sv_ref[it:it + 1, :] = m
        hit = cur >= m
        if want_rank:
            rank = jnp.where(hit, float(it), rank)
        cur = jnp.where(hit, NEG, cur)
    return rank


def _peera_kernel(hq_ref, wq_ref, keys_ref, cnt_ref, coef_ref, rank_ref, e2_ref,
                  sv1_scr, sv2_scr, cand_scr, svc_scr):
    k = PEER_TOPK
    n_heads = keys_ref.shape[0]
    dk = keys_ref.shape[3]
    q_t = _nt_dot(wq_ref[...], hq_ref[...])
    for h in range(n_heads):
        s_t = []
        for s in range(2):
            r0 = (2 * h + s) * dk
            s_t.append(jnp.dot(keys_ref[h, s], q_t[r0:r0 + dk, :].astype(BF16),
                               preferred_element_type=F32))
        _top_values(s_t[0], k + 1, sv1_scr)
        rank2 = _top_values(s_t[1], k + 1, sv2_scr, want_rank=True)
        sv2 = sv2_scr[0:k, :]
        for a in range(k):
            cand_scr[a * k:(a + 1) * k, :] = sv1_scr[a:a + 1, :] + sv2
        cand = cand_scr[...]
        _top_values(cand, k + 1, svc_scr)
        c_next = jnp.maximum(svc_scr[k:k + 1, :],
                             jnp.maximum(sv1_scr[k:k + 1, :] + sv2_scr[0:1, :],
                                         sv1_scr[0:1, :] + sv2_scr[k:k + 1, :]))
        tau = 0.5 * (svc_scr[k - 1:k, :] + c_next)
        top = sv1_scr[0:1, :] + sv2_scr[0:1, :]
        z = jnp.sum(jnp.where(cand > tau, jnp.exp(cand - top), 0.0), axis=0, keepdims=True)
        thr = tau - s_t[0]
        count = jnp.zeros_like(thr)
        for b in range(k):
            count = count + jnp.where(sv2_scr[b:b + 1, :] >= thr, 1.0, 0.0)
        cnt_ref[h] = count
        coef_ref[h] = jnp.exp(s_t[0] - sv1_scr[0:1, :]) / z
        rank_ref[h] = rank2.astype(BF16)
        e2_ref[h] = jnp.exp(s_t[1] - sv2_scr[0:1, :]).astype(BF16)


def _peera(hq, wq_t, keys):
    t, d = hq.shape
    n_heads, _, n_keys, _ = keys.shape
    tt = min(512, t)
    k = PEER_TOPK
    shape = (n_heads, n_keys, t)
    ospec = pl.BlockSpec((n_heads, n_keys, tt), lambda i: (0, 0, i))
    return pl.pallas_call(
        _peera_kernel,
        grid=(t // tt,),
        in_specs=[pl.BlockSpec((tt, d), lambda i: (i, 0)),
                  _resident(wq_t.shape, lambda i: (0, 0)),
                  pl.BlockSpec(keys.shape, lambda i: (0, 0, 0, 0))],
        out_specs=[ospec] * 4,
        out_shape=[jax.ShapeDtypeStruct(shape, F32), jax.ShapeDtypeStruct(shape, F32),
                   jax.ShapeDtypeStruct(shape, BF16), jax.ShapeDtypeStruct(shape, BF16)],
        scratch_shapes=[pltpu.VMEM((k + 8, tt), F32), pltpu.VMEM((k + 8, tt), F32),
                        pltpu.VMEM((k * k, tt), F32), pltpu.VMEM((k + 8, tt), F32)],
        compiler_params=_cparams(1),
        name="peera",
    )(hq, wq_t, keys)


def _peerb_kernel(hq_ref, cnt_ref, coef_ref, rank_ref, e2_ref, u_ref, vt_ref, x1_ref, g2_ref, fg_ref,
                  o_ref, acc_scr, a0_scr, a1_scr, wg_new, row_scr, *, n_eb):
    e = pl.program_id(1)
    n_heads, n_keys, _ = rank_ref.shape

    @pl.when((pl.program_id(0) == 0) & (e == 0))
    def _():
        a1_scr[...] = jnp.zeros_like(a1_scr)

    @pl.when(e == 0)
    def _():
        acc_scr[...] = jnp.zeros_like(acc_scr)

    def stages(a_new, a_old):
        half = jnp.where(e >= 1, 0.5, 0.0)
        n_i = cnt_ref.shape[1]
        for h in range(n_heads):
            for ii in range(n_i):
                for tab, ref in enumerate((cnt_ref, coef_ref)):
                    row = jnp.broadcast_to(ref[h, ii:ii + 1, :], (BF16_ROWS, ref.shape[2]))
                    row_scr[tab, h * n_i + ii] = row.astype(BF16)

        def gated_weights(sub):
            lanes = slice(sub * LANE, (sub + 1) * LANE)
            n_jc = n_keys // BF16_ROWS
            for ii in range(n_i):
                w = [None] * n_jc
                for h in range(n_heads):
                    cnt = row_scr[0, h * n_i + ii, :, lanes]
                    coef = row_scr[1, h * n_i + ii, :, lanes]
                    for jc in range(n_jc):
                        jrows = slice(jc * BF16_ROWS, (jc + 1) * BF16_ROWS)
                        sel = jnp.clip(cnt - rank_ref[h, jrows, lanes], 0.0, 1.0)
                        term = (sel * e2_ref[h, jrows, lanes]) * coef
                        w[jc] = term if w[jc] is None else w[jc] + term
                for jc in range(n_jc):
                    rows = slice(ii * n_keys + jc * BF16_ROWS, ii * n_keys + (jc + 1) * BF16_ROWS)
                    act = a_old[rows, lanes]
                    gelu = (half * act) * (1.0 + lax.erf(act * (1.0 / math.sqrt(2.0))))
                    wg_new[rows, lanes] = w[jc] * gelu.astype(BF16)

        tt = rank_ref.shape[2]
        piece = min(MXU_WIDTH, tt)
        subs = piece // LANE
        for pc in range(tt // piece):
            cols = slice(pc * piece, (pc + 1) * piece)
            for sub in range(pc * subs, pc * subs + (subs + 1) // 2):
                gated_weights(sub)
            a_new[:, cols] = _nt_dot(u_ref[...], hq_ref[cols, :])
            for sub in range(pc * subs + (subs + 1) // 2, (pc + 1) * subs):
                gated_weights(sub)
            acc_scr[:, cols] += jnp.dot(vt_ref[...], wg_new[:, cols], preferred_element_type=F32)

    @pl.when(e % 2 == 0)
    def _():
        stages(a0_scr, a1_scr)

    @pl.when(e % 2 == 1)
    def _():
        stages(a1_scr, a0_scr)

    @pl.when(e == n_eb)
    def _():
        x2 = x1_ref[...] + g2_ref[0] * acc_scr[...].T
        o_ref[...] = _rms(x2, fg_ref[...])


def _peerb(hq, cnt, coef, rank, e2, u, v, x1, g2, final_g, seq_len):
    t, d = hq.shape
    n_heads, n_keys, _ = rank.shape
    n_exp = u.shape[0]
    tt = min(512, t)
    i_blk = 8
    et = i_blk * n_keys
    per_batch = seq_len // tt
    n_eb = n_exp // et
    v_t = v.reshape(n_eb, et, d).transpose(0, 2, 1)
    blk = lambda e, lag: jnp.clip(e - lag, 0, n_eb - 1)
    stat = pl.BlockSpec((n_heads, n_keys, tt), lambda i, e: (0, 0, i), pipeline_mode=pl.Buffered(1))
    rowstat = pl.BlockSpec((n_heads, i_blk, tt), lambda i, e: (0, blk(e, 1), i))
    kern = functools.partial(_peerb_kernel, n_eb=n_eb)
    return pl.pallas_call(
        kern,
        grid=(t // tt, n_eb + 1),
        in_specs=[pl.BlockSpec((tt, d), lambda i, e: (i, 0)),
                  rowstat, rowstat, stat, stat,
                  pl.BlockSpec((et, d), lambda i, e: (blk(e, 0), 0)),
                  pl.BlockSpec((None, d, et), lambda i, e: (blk(e, 1), 0, 0)),
                  pl.BlockSpec((tt, d), lambda i, e: (i, 0), pipeline_mode=pl.Buffered(1)),
                  pl.BlockSpec((1, 1, d), lambda i, e: (i // per_batch, 0, 0)),
                  pl.BlockSpec((1, d), lambda i, e: (0, 0))],
        out_specs=pl.BlockSpec((tt, d), lambda i, e: (i, 0)),
        out_shape=jax.ShapeDtypeStruct((t, d), F32),
        scratch_shapes=[pltpu.VMEM((d, tt), F32),
                        pltpu.VMEM((et, tt), F32), pltpu.VMEM((et, tt), F32),
                        pltpu.VMEM((et, tt), BF16),
                        pltpu.VMEM((2, n_heads * i_blk, BF16_ROWS, tt), BF16)],
        compiler_params=_cparams(2),
        name="peerb",
    )(hq, cnt, coef, rank, e2, u, v_t, x1, g2, final_g.reshape(1, d))


def kernel(x, c, ctx, c_ctx, w_ada, b_ada, norm1_g, norm2_g, w_in, ssd_conv_w, ssd_conv_b, ssd_dt_bias,
           ssd_A_log, ssd_D, ssd_norm_g, ssd_w_out, sc_conv_w, sc_w_out, w_o, peer_w_q, peer_keys,
           peer_u, peer_v, final_g):
    batch, seq_len, d = x.shape
    ctx_len = ctx.shape[1]
    depth = w_in.shape[0]
    assert depth == 1, "single-layer block"
    heads = ssd_D.shape[-1]
    d_ssd = ssd_norm_g.shape[-1]
    n_xbc = ssd_conv_w.shape[-1]
    gn = (n_xbc - d_ssd) // 2
    groups = gn // SSD_STATE
    d_sc = sc_conv_w.shape[-1]
    assert d_ssd // heads == LANE // 2 and heads % (2 * groups) == 0 and 2 * heads <= LANE
    assert d_sc == d and n_xbc % d == 0 and d_ssd % d == 0 and d % LANE == 0
    assert seq_len % SSD_CHUNK == 0 and ctx_len % SSD_CHUNK == 0

    off_dt = n_xbc
    off_z = off_dt + 2 * heads
    off_sc = off_z + d_ssd
    off_gate = off_sc + 3 * d_sc
    w_in0 = w_in[0]
    w_main = jnp.concatenate([w_in0[:, off_z:off_sc], w_in0[:, :off_dt], w_in0[:, off_sc:]],
                             axis=1).astype(BF16)
    w_main = w_main.reshape(d, -1, d).transpose(1, 0, 2)
    w_dt = jnp.pad(w_in0[:, off_dt:off_z], ((0, 0), (0, LANE - 2 * heads))).astype(BF16)
    dt_bias = jnp.pad(ssd_dt_bias[0].reshape(1, 2 * heads), ((0, 0), (0, LANE - 2 * heads)))
    a_log = jnp.pad(ssd_A_log[0].reshape(1, 2 * heads).astype(F32), ((0, 0), (0, LANE - 2 * heads)))
    xbc_blk = d_ssd // d
    scb_blk = (d_ssd + n_xbc) // d
    gate_blk = scb_blk + 3
    assert off_gate - off_sc == 3 * d

    rows = -(-(batch + 1) // 8) * 8
    cvec = jnp.zeros((rows, d), F32).at[:batch].set(c).at[batch].set(c_ctx)
    mod = _ada(cvec, w_ada[0], b_ada[0])
    chunk = lambda r, i: mod[r, i * d:(i + 1) * d]
    lat = lambda i: mod[:batch, i * d:(i + 1) * d].reshape(batch, 1, d)
    sh1, sc1, g1, sh2, sc2, g2 = (lat(i) for i in range(6))
    sh_c = jnp.broadcast_to(chunk(batch, 0), (batch, 1, d))
    sc_c = jnp.broadcast_to(chunk(batch, 1), (batch, 1, d))

    conv_w = ssd_conv_w[0]
    conv_b = ssd_conv_b[0]
    p_c, dt_c = _inproj(ctx.reshape(batch * ctx_len, d), norm1_g[0], sh_c, sc_c,
                        w_main[xbc_blk:xbc_blk + n_xbc // d], w_dt, ctx_len)
    x3_c = _ssd_conv(p_c, conv_w, conv_b, batch, ctx_len, n_xbc, 0)
    dt_c = dt_c.reshape(batch, ctx_len, LANE)
    zero_state = jnp.zeros((batch, heads // 2, LANE, SSD_STATE), F32)
    _, hf_c = _ssd_scan(x3_c, dt_c, dt_bias, a_log, zero_state, heads, groups, False)
    _, hb_c = _ssd_scan(x3_c, dt_c, dt_bias, a_log, zero_state, heads, groups, True)

    x2d = x.reshape(batch * seq_len, d)
    p, dt_l = _inproj(x2d, norm1_g[0], sh1, sc1, w_main, w_dt, seq_len)
    x3 = _ssd_conv(p, conv_w, conv_b, batch, seq_len, n_xbc, xbc_blk)
    dt_l = dt_l.reshape(batch, seq_len, LANE)
    drow = jnp.repeat(ssd_D[0], LANE // 2).reshape(heads // 2, LANE)
    y_f, _ = _ssd_scan(x3, dt_l, dt_bias, a_log, hf_c, heads, groups, False)
    y, _ = _ssd_scan(x3, dt_l, dt_bias, a_log, hb_c, heads, groups, True, drow=drow, yprev=y_f)

    m1 = _mixa(y.reshape(batch * seq_len, d_ssd), p, ssd_norm_g[0], ssd_w_out[0].astype(BF16), gate_blk)
    x1, hq = _mixb(p, m1, x2d, g1, sc_conv_w[0], sc_w_out[0].astype(BF16), w_o[0].astype(BF16),
                   norm2_g[0], sh2, sc2, seq_len, scb_blk, gate_blk + 1)

    cnt, coef, rank, e2 = _peera(hq, peer_w_q[0].T.astype(BF16), peer_keys[0].astype(BF16))
    out = _peerb(hq, cnt, coef, rank, e2, peer_u[0].astype(BF16), peer_v[0].astype(BF16), x1, g2,
                 final_g, seq_len)
    return out.reshape(batch, seq_len, d)
```

```python
import functools
import math

import jax
import jax.numpy as jnp
from jax import lax
from jax.experimental import pallas as pl
from jax.experimental.pallas import tpu as pltpu

F32 = jnp.float32
BF16 = jnp.bfloat16

LANE = 128
BF16_ROWS = 16
MXU_WIDTH = 256
VMEM_LIMIT_BYTES = 56 * 1024 * 1024

EPS = 1e-6
SSD_CHUNK = 128
SSD_STATE = 128
GRID_W = 64
PEER_TOPK = 16
NEG = -1e30

HIGHEST = lax.Precision.HIGHEST


def _cparams(n_axes, flags=None):
    return pltpu.CompilerParams(dimension_semantics=("arbitrary",) * n_axes,
                                vmem_limit_bytes=VMEM_LIMIT_BYTES, flags=flags)


def _resident(block_shape, index_map):
    return pl.BlockSpec(block_shape, index_map, pipeline_mode=pl.Buffered(1))


def _silu(v):
    return v * jax.nn.sigmoid(v)


def _rms(v, g):
    return v * lax.rsqrt(jnp.mean(v * v, axis=-1, keepdims=True) + EPS) * g


def _nt_dot(a, b):
    return lax.dot_general(a, b, (((1,), (1,)), ((), ())), preferred_element_type=F32)


def _ada_kernel(c_ref, w_ref, b_ref, o_ref):
    s = _silu(c_ref[...])
    o_ref[...] = jnp.dot(s, w_ref[...], preferred_element_type=F32, precision=HIGHEST) + b_ref[...]


def _ada(cvec, w, b):
    rows, d = cvec.shape
    n = w.shape[1]
    tn = 1024 if n % 1024 == 0 else d
    return pl.pallas_call(
        _ada_kernel,
        grid=(n // tn,),
        in_specs=[pl.BlockSpec((rows, d), lambda j: (0, 0)),
                  pl.BlockSpec((d, tn), lambda j: (0, j)),
                  pl.BlockSpec((1, tn), lambda j: (0, j))],
        out_specs=pl.BlockSpec((rows, tn), lambda j: (0, j)),
        out_shape=jax.ShapeDtypeStruct((rows, n), F32),
        compiler_params=_cparams(1),
        name="ada",
    )(cvec, w, b.reshape(1, n))


def _inproj_kernel(x_ref, g_ref, sh_ref, sc_ref, w_ref, wdt_ref, p_ref, dt_ref, h_scr):
    @pl.when(pl.program_id(1) == 0)
    def _():
        h = _rms(x_ref[...], g_ref[...]) * (1.0 + sc_ref[0]) + sh_ref[0]
        hb = h.astype(BF16)
        h_scr[...] = hb
        dt_ref[...] = jnp.dot(hb, wdt_ref[...], preferred_element_type=F32)

    p_ref[...] = jnp.dot(h_scr[...], w_ref[...], preferred_element_type=F32).astype(p_ref.dtype)


def _inproj(x2d, norm_g, shift, scale, w_main, w_dt, seq_len):
    t, d = x2d.shape
    n_blk, _, tn = w_main.shape
    tm = min(512, seq_len)
    per_batch = seq_len // tm
    return pl.pallas_call(
        _inproj_kernel,
        grid=(t // tm, n_blk),
        in_specs=[pl.BlockSpec((tm, d), lambda i, j: (i, 0)),
                  pl.BlockSpec((1, d), lambda i, j: (0, 0)),
                  pl.BlockSpec((1, 1, d), lambda i, j: (i // per_batch, 0, 0)),
                  pl.BlockSpec((1, 1, d), lambda i, j: (i // per_batch, 0, 0)),
                  pl.BlockSpec((None, d, tn), lambda i, j: (j, 0, 0)),
                  pl.BlockSpec((d, LANE), lambda i, j: (0, 0))],
        out_specs=[pl.BlockSpec((None, tm, tn), lambda i, j: (j, i, 0)),
                   pl.BlockSpec((tm, LANE), lambda i, j: (i, 0))],
        out_shape=[jax.ShapeDtypeStruct((n_blk, t, tn), BF16),
                   jax.ShapeDtypeStruct((t, LANE), F32)],
        scratch_shapes=[pltpu.VMEM((tm, d), BF16)],
        compiler_params=_cparams(2),
        name="inproj",
    )(x2d, norm_g.reshape(1, d), shift, scale, w_main, w_dt)


def _conv_kernel(main_ref, prev_ref, next_ref, w_ref, b_ref, o_ref, e_scr, *, rows, n_rb, taps):
    rb = pl.program_id(1)
    halo = BF16_ROWS
    pad = taps // 2
    e_scr[halo:halo + rows, :] = main_ref[...].astype(F32)
    e_scr[0:halo, :] = jnp.where(rb > 0, prev_ref[...].astype(F32), 0.0)
    e_scr[halo + rows:2 * halo + rows, :] = jnp.where(rb < n_rb - 1, next_ref[...].astype(F32), 0.0)
    acc = b_ref[...] + w_ref[0:1, :] * e_scr[halo - pad:halo - pad + rows, :]
    for k in range(1, taps):
        acc = acc + w_ref[k:k + 1, :] * e_scr[halo - pad + k:halo - pad + k + rows, :]
    act = _silu(acc).astype(o_ref.dtype)
    for t in range(o_ref.shape[1]):
        o_ref[0, t] = act[:, t * LANE:(t + 1) * LANE]


def _ssd_conv(p3, conv_w, conv_b, batch, seq_len, n_xbc, col_blk):
    tc = p3.shape[2]
    taps = conv_w.shape[0]
    rows = min(512, seq_len)
    n_rb = seq_len // rows
    n_col = n_xbc // tc
    halo = BF16_ROWS
    blocks_per_rb = rows // halo
    last_halo_block = batch * seq_len // halo - 1
    kern = functools.partial(_conv_kernel, rows=rows, n_rb=n_rb, taps=taps)
    return pl.pallas_call(
        kern,
        grid=(batch, n_rb, n_col),
        in_specs=[
            pl.BlockSpec((None, rows, tc), lambda b, r, j: (j + col_blk, b * n_rb + r, 0)),
            pl.BlockSpec((None, halo, tc),
                         lambda b, r, j: (j + col_blk, jnp.maximum((b * n_rb + r) * blocks_per_rb - 1, 0), 0)),
            pl.BlockSpec((None, halo, tc),
                         lambda b, r, j: (j + col_blk,
                                          jnp.minimum((b * n_rb + r + 1) * blocks_per_rb, last_halo_block), 0)),
            pl.BlockSpec((taps, tc), lambda b, r, j: (0, j)),
            pl.BlockSpec((1, tc), lambda b, r, j: (0, j)),
        ],
        out_specs=pl.BlockSpec((1, tc // LANE, rows, LANE), lambda b, r, j: (b, j, r, 0)),
        out_shape=jax.ShapeDtypeStruct((batch, n_xbc // LANE, seq_len, LANE), BF16),
        scratch_shapes=[pltpu.VMEM((rows + 2 * halo, tc), F32)],
        compiler_params=_cparams(3),
        name="ssdconv",
    )(p3, p3, p3, conv_w, conv_b.reshape(1, n_xbc))


def _ssd_kernel(*refs, reverse, add_prev, n_chunks, heads, groups):
    if add_prev:
        (x3_ref, dtr_ref, bias_ref, alog_ref, h0_ref, drow_ref, yprev_ref,
         y_ref, hfin_ref, h_scr) = refs
    else:
        x3_ref, dtr_ref, bias_ref, alog_ref, h0_ref, y_ref, hfin_ref, h_scr = refs
    q = SSD_CHUNK
    n_pairs = heads // 2
    pairs_per_group = n_pairs // groups
    col0 = heads if reverse else 0
    step = pl.program_id(1)

    @pl.when(step == 0)
    def _():
        h_scr[...] = h0_ref[0]

    pre = dtr_ref[0] + bias_ref[...]
    dt = jnp.maximum(pre, 0.0) + jnp.log1p(jnp.exp(-jnp.abs(pre)))
    a = dt * (-jnp.exp(alog_ref[...]))
    ri = lax.broadcasted_iota(jnp.int32, (q, q), 0)
    ci = lax.broadcasted_iota(jnp.int32, (q, q), 1)
    tri = (ci >= ri) if reverse else (ci <= ri)
    tri_t = (ri >= ci) if reverse else (ri <= ci)
    acum = jnp.dot(tri.astype(F32), a, preferred_element_type=F32, precision=HIGHEST)
    a_t = a.T
    dt_t = dt.T
    acum_t = jnp.dot(a_t, tri_t.astype(F32), preferred_element_type=F32, precision=HIGHEST)
    tot = jnp.dot(a_t, jnp.ones((q, q), F32), preferred_element_type=F32, precision=HIGHEST)
    w_t = dt_t * jnp.exp(tot - acum_t)
    dec_t = jnp.exp(tot)
    low_lanes = ci < (LANE // 2)
    low_rows = ri < (LANE // 2)

    cb = None
    for kp in range(n_pairs):
        g = kp // pairs_per_group
        b_g = x3_ref[0, n_pairs + g]
        c_g = x3_ref[0, n_pairs + groups + g]
        if kp % pairs_per_group == 0:
            cb = _nt_dot(c_g, b_g)
            c_f = c_g.astype(F32)
        ms, cds = [], []
        for hc in (col0 + 2 * kp, col0 + 2 * kp + 1):
            col = jnp.broadcast_to(acum[:, hc:hc + 1], (q, q))
            seg = col - acum_t[hc:hc + 1, :]
            decay = jnp.exp(jnp.where(tri, seg, NEG))
            ms.append((cb * decay * dt_t[hc:hc + 1, :]).astype(BF16))
            cds.append((c_f * jnp.exp(col)).astype(BF16))
        xp = x3_ref[0, kp].astype(F32)
        rhs = jnp.concatenate([jnp.where(low_lanes, xp, 0.0), jnp.where(low_lanes, 0.0, xp)],
                              axis=0).astype(BF16)
        y = jnp.dot(jnp.concatenate(ms, axis=1), rhs, preferred_element_type=F32)
        hp = h_scr[kp]
        rhs2 = jnp.concatenate([jnp.where(low_rows, hp, 0.0), jnp.where(low_rows, 0.0, hp)],
                               axis=1).astype(BF16)
        y = y + _nt_dot(jnp.concatenate(cds, axis=1), rhs2)
        if add_prev:
            y = y + drow_ref[kp:kp + 1, :] * xp + yprev_ref[0, :, kp * LANE:(kp + 1) * LANE].astype(F32)
        y_ref[0, :, kp * LANE:(kp + 1) * LANE] = y.astype(y_ref.dtype)
        h1, h2 = col0 + 2 * kp, col0 + 2 * kp + 1
        w_sel = jnp.where(low_rows, w_t[h1:h1 + 1, :], w_t[h2:h2 + 1, :])
        d_sel = jnp.where(low_rows, dec_t[h1:h1 + 1, :], dec_t[h2:h2 + 1, :])
        xw_t = (xp.T * w_sel).astype(BF16)
        h_scr[kp] = d_sel * hp + jnp.dot(xw_t, b_g, preferred_element_type=F32)

    @pl.when(step == n_chunks - 1)
    def _():
        hfin_ref[0] = h_scr[...]


def _ssd_scan(x3, dtraw, dt_bias, a_log, h0, heads, groups, reverse, drow=None, yprev=None):
    batch, n_tiles, seq_len, _ = x3.shape
    n_chunks = seq_len // SSD_CHUNK
    n_pairs = heads // 2
    d_ssd = n_pairs * LANE
    add_prev = yprev is not None
    if reverse:
        chunk = lambda k: n_chunks - 1 - k
    else:
        chunk = lambda k: k
    in_specs = [
        pl.BlockSpec((1, n_tiles, SSD_CHUNK, LANE), lambda b, k: (b, 0, chunk(k), 0)),
        pl.BlockSpec((1, SSD_CHUNK, LANE), lambda b, k: (b, chunk(k), 0)),
        pl.BlockSpec((1, LANE), lambda b, k: (0, 0)),
        pl.BlockSpec((1, LANE), lambda b, k: (0, 0)),
        pl.BlockSpec((1, n_pairs, LANE, SSD_STATE), lambda b, k: (b, 0, 0, 0)),
    ]
    args = [x3, dtraw, dt_bias, a_log, h0]
    if add_prev:
        in_specs += [pl.BlockSpec((n_pairs, LANE), lambda b, k: (0, 0)),
                     pl.BlockSpec((1, SSD_CHUNK, d_ssd), lambda b, k: (b, chunk(k), 0))]
        args += [drow, yprev]
    kern = functools.partial(_ssd_kernel, reverse=reverse, add_prev=add_prev, n_chunks=n_chunks,
                             heads=heads, groups=groups)
    return pl.pallas_call(
        kern,
        grid=(batch, n_chunks),
        in_specs=in_specs,
        out_specs=[pl.BlockSpec((1, SSD_CHUNK, d_ssd), lambda b, k: (b, chunk(k), 0)),
                   pl.BlockSpec((1, n_pairs, LANE, SSD_STATE), lambda b, k: (b, 0, 0, 0))],
        out_shape=[jax.ShapeDtypeStruct((batch, seq_len, d_ssd), BF16),
                   jax.ShapeDtypeStruct((batch, n_pairs, LANE, SSD_STATE), F32)],
        scratch_shapes=[pltpu.VMEM((n_pairs, LANE, SSD_STATE), F32)],
        compiler_params=_cparams(2),
        name="ssd_bwd" if reverse else "ssd_fwd",
    )(*args)


def _mixa_kernel(*refs):
    y_ref, *z_refs, gate_ref, ng_ref, w_ref, o_ref = refs
    z = jnp.concatenate([r[...] for r in z_refs], axis=1).astype(F32)
    v = _rms(y_ref[...].astype(F32) * _silu(z), ng_ref[...])
    o = jnp.dot(v.astype(BF16), w_ref[...], preferred_element_type=F32)
    o_ref[...] = (jax.nn.sigmoid(gate_ref[...].astype(F32)) * o).astype(o_ref.dtype)


def _mixa(y2d, p3, norm_g, w_out, gate_blk):
    t, d_ssd = y2d.shape
    d = w_out.shape[1]
    tm = min(256, t)
    n_z = d_ssd // d
    return pl.pallas_call(
        _mixa_kernel,
        grid=(t // tm,),
        in_specs=[pl.BlockSpec((tm, d_ssd), lambda i: (i, 0))]
                 + [pl.BlockSpec((None, tm, d), functools.partial(lambda k, i: (k, i, 0), k))
                    for k in range(n_z)]
                 + [pl.BlockSpec((None, tm, d), lambda i: (gate_blk, i, 0)),
                  pl.BlockSpec((1, d_ssd), lambda i: (0, 0)),
                  _resident((d_ssd, d), lambda i: (0, 0))],
        out_specs=pl.BlockSpec((tm, d), lambda i: (i, 0)),
        out_shape=jax.ShapeDtypeStruct((t, d), BF16),
        compiler_params=_cparams(1),
        name="mixa",
    )(y2d, *([p3] * (n_z + 1)), norm_g.reshape(1, d_ssd), w_out)


def _mixb_kernel(scb_ref, scc_ref, scx_ref, gsc_ref, m1_ref, x_ref, g1_ref, cw_ref, wsc_ref, wo_ref,
                 n2_ref, sh2_ref, sc2_ref, x1_ref, hq_ref, u_scr, *, rows):
    pad = 8
    u_scr[0:pad, :] = jnp.zeros((pad, u_scr.shape[1]), F32)
    u_scr[pad + rows:2 * pad + rows, :] = jnp.zeros((pad, u_scr.shape[1]), F32)
    u_scr[pad:pad + rows, :] = scc_ref[...].astype(F32) * scx_ref[...].astype(F32)
    pos = jnp.bitwise_and(lax.broadcasted_iota(jnp.int32, (rows, u_scr.shape[1]), 0), GRID_W - 1)
    left = jnp.where(pos == 0, 0.0, u_scr[pad - 1:pad - 1 + rows, :])
    right = jnp.where(pos == GRID_W - 1, 0.0, u_scr[pad + 1:pad + 1 + rows, :])
    conv = cw_ref[0:1, :] * left + cw_ref[1:2, :] * u_scr[pad:pad + rows, :] + cw_ref[2:3, :] * right
    v = (scb_ref[...].astype(F32) * conv).astype(BF16)
    y_sc = jnp.dot(v, wsc_ref[...], preferred_element_type=F32)
    merged = m1_ref[...].astype(F32) + jax.nn.sigmoid(gsc_ref[...].astype(F32)) * y_sc
    out = jnp.dot(merged.astype(BF16), wo_ref[...], preferred_element_type=F32)
    x1 = x_ref[...] + g1_ref[0] * out
    x1_ref[...] = x1
    hq_ref[...] = (_rms(x1, n2_ref[...]) * (1.0 + sc2_ref[0]) + sh2_ref[0]).astype(hq_ref.dtype)


def _mixb(p3, m1, x2d, g1, sc_conv_w, w_sc, w_o, norm2_g, sh2, sc2, seq_len, scb_blk, gsc_blk):
    t, d = x2d.shape
    tm = min(256, t)
    per_batch = seq_len // tm
    bmap = lambda i: (i // per_batch, 0, 0)
    tile = lambda blk: pl.BlockSpec((tm, d), lambda i: (i, blk))
    ptile = lambda blk: pl.BlockSpec((None, tm, d), lambda i: (blk, i, 0))
    kern = functools.partial(_mixb_kernel, rows=tm)
    return pl.pallas_call(
        kern,
        grid=(t // tm,),
        in_specs=[ptile(scb_blk), ptile(scb_blk + 1), ptile(scb_blk + 2), ptile(gsc_blk),
                  tile(0), tile(0),
                  pl.BlockSpec((1, 1, d), bmap),
                  pl.BlockSpec(sc_conv_w.shape, lambda i: (0, 0)),
                  _resident((d, d), lambda i: (0, 0)),
                  _resident((d, d), lambda i: (0, 0)),
                  pl.BlockSpec((1, d), lambda i: (0, 0)),
                  pl.BlockSpec((1, 1, d), bmap),
                  pl.BlockSpec((1, 1, d), bmap)],
        out_specs=[tile(0), tile(0)],
        out_shape=[jax.ShapeDtypeStruct((t, d), F32), jax.ShapeDtypeStruct((t, d), BF16)],
        scratch_shapes=[pltpu.VMEM((tm + 16, d), F32)],
        compiler_params=_cparams(1),
        name="mixb",
    )(p3, p3, p3, p3, m1, x2d, g1, sc_conv_w, w_sc, w_o, norm2_g.reshape(1, d), sh2, sc2)


def _top_values(cur, n, sv_ref):
    for it in range(n):
        m = jnp.max(cur, axis=0, keepdims=True)
        sv_ref[it:it + 1, :] = m
        cur = jnp.where(cur >= m, NEG, cur)


def _peera_kernel(hq_ref, wq_ref, keys_ref, thr_ref, coef_ref, s2_ref, e2_ref,
                  sv1_scr, sv2_scr, cand_scr, svc_scr):
    k = PEER_TOPK
    sub = 8
    n_heads = keys_ref.shape[0]
    dk = keys_ref.shape[3]
    q_t = _nt_dot(wq_ref[...], hq_ref[...])
    row = lax.broadcasted_iota(jnp.int32, (sub, q_t.shape[1]), 0)
    for h in range(n_heads):
        s_t = []
        for s in range(2):
            r0 = (2 * h + s) * dk
            s_t.append(jnp.dot(keys_ref[h, s], q_t[r0:r0 + dk, :].astype(BF16),
                               preferred_element_type=F32))
        _top_values(s_t[0], k + 1, sv1_scr)
        _top_values(s_t[1], k + 1, sv2_scr)
        cand_scr[0:k, :] = sv1_scr[0:k, :] + sv2_scr[0:1, :]
        for b in range(1, sub):
            blk = sv1_scr[0:sub, :] + sv2_scr[b:b + 1, :]
            cand_scr[k + (b - 1) * sub:k + b * sub, :] = jnp.where(row < k // (b + 1), blk, NEG)
        cand_scr[k + (sub - 1) * sub:k + sub * sub, :] = sv2_scr[sub:k, :] + sv1_scr[0:1, :]
        cand = cand_scr[...]
        _top_values(cand, k + 1, svc_scr)
        c_next = jnp.maximum(svc_scr[k:k + 1, :],
                             jnp.maximum(sv1_scr[k:k + 1, :] + sv2_scr[0:1, :],
                                         sv1_scr[0:1, :] + sv2_scr[k:k + 1, :]))
        tau = 0.5 * (svc_scr[k - 1:k, :] + c_next)
        top = sv1_scr[0:1, :] + sv2_scr[0:1, :]
        z = jnp.sum(jnp.where(cand > tau, jnp.exp(cand - top), 0.0), axis=0, keepdims=True)
        thr_ref[h] = tau - s_t[0]
        coef_ref[h] = jnp.exp(s_t[0] - sv1_scr[0:1, :]) / z
        s2_ref[h] = s_t[1]
        e2_ref[h] = jnp.exp(s_t[1] - sv2_scr[0:1, :])


def _peera(hq, wq_t, keys):
    t, d = hq.shape
    n_heads, _, n_keys, _ = keys.shape
    tt = min(512, t)
    k = PEER_TOPK
    n_cand = k + 64
    out = jax.ShapeDtypeStruct((n_heads, n_keys, t), F32)
    ospec = pl.BlockSpec((n_heads, n_keys, tt), lambda i: (0, 0, i))
    return pl.pallas_call(
        _peera_kernel,
        grid=(t // tt,),
        in_specs=[pl.BlockSpec((tt, d), lambda i: (i, 0)),
                  _resident(wq_t.shape, lambda i: (0, 0)),
                  pl.BlockSpec(keys.shape, lambda i: (0, 0, 0, 0))],
        out_specs=[ospec] * 4,
        out_shape=[out] * 4,
        scratch_shapes=[pltpu.VMEM((k + 8, tt), F32), pltpu.VMEM((k + 8, tt), F32),
                        pltpu.VMEM((n_cand, tt), F32), pltpu.VMEM((k + 8, tt), F32)],
        compiler_params=_cparams(1),
        name="peera",
    )(hq, wq_t, keys)


def _peerb_kernel(hq_ref, thr_ref, coef_ref, s2_ref, e2_ref, u_ref, vt_ref, x1_ref, g2_ref, fg_ref,
                  o_ref, acc_scr, a0_scr, a1_scr, wg_new, *, n_eb):
    e = pl.program_id(1)
    n_heads, n_keys, _ = s2_ref.shape

    @pl.when((pl.program_id(0) == 0) & (e == 0))
    def _():
        a1_scr[...] = jnp.zeros_like(a1_scr)

    @pl.when(e == 0)
    def _():
        acc_scr[...] = jnp.zeros_like(acc_scr)

    def stages(a_new, a_old):
        half = jnp.where(e >= 1, 0.5, 0.0)
        n_i = thr_ref.shape[1]

        def gated_weights(sub):
            lanes = slice(sub * LANE, (sub + 1) * LANE)
            n_jc = n_keys // BF16_ROWS
            for ii in range(n_i):
                w = [None] * n_jc
                for h in range(n_heads):
                    thr = jnp.broadcast_to(thr_ref[h, ii:ii + 1, lanes], (BF16_ROWS, LANE))
                    coef = jnp.broadcast_to(coef_ref[h, ii:ii + 1, lanes], (BF16_ROWS, LANE))
                    for jc in range(n_jc):
                        jrows = slice(jc * BF16_ROWS, (jc + 1) * BF16_ROWS)
                        term = coef * jnp.where(s2_ref[h, jrows, lanes] >= thr, e2_ref[h, jrows, lanes], 0.0)
                        w[jc] = term if w[jc] is None else w[jc] + term
                for jc in range(n_jc):
                    rows = slice(ii * n_keys + jc * BF16_ROWS, ii * n_keys + (jc + 1) * BF16_ROWS)
                    act = a_old[rows, lanes]
                    gelu = (half * act) * (1.0 + lax.erf(act * (1.0 / math.sqrt(2.0))))
                    wg_new[rows, lanes] = (w[jc] * gelu).astype(BF16)

        tt = s2_ref.shape[2]
        piece = min(MXU_WIDTH, tt)
        subs = piece // LANE
        for pc in range(tt // piece):
            cols = slice(pc * piece, (pc + 1) * piece)
            for sub in range(pc * subs, pc * subs + (subs + 1) // 2):
                gated_weights(sub)
            a_new[:, cols] = _nt_dot(u_ref[...], hq_ref[cols, :])
            for sub in range(pc * subs + (subs + 1) // 2, (pc + 1) * subs):
                gated_weights(sub)
            acc_scr[:, cols] += jnp.dot(vt_ref[...], wg_new[:, cols], preferred_element_type=F32)

    @pl.when(e % 2 == 0)
    def _():
        stages(a0_scr, a1_scr)

    @pl.when(e % 2 == 1)
    def _():
        stages(a1_scr, a0_scr)

    @pl.when(e == n_eb)
    def _():
        x2 = x1_ref[...] + g2_ref[0] * acc_scr[...].T
        o_ref[...] = _rms(x2, fg_ref[...])


def _peerb(hq, thr, coef, s2, e2, u, v, x1, g2, final_g, seq_len):
    t, d = hq.shape
    n_heads, n_keys, _ = s2.shape
    n_exp = u.shape[0]
    tt = min(512, t)
    i_blk = 8
    et = i_blk * n_keys
    per_batch = seq_len // tt
    n_eb = n_exp // et
    v_t = v.reshape(n_eb, et, d).transpose(0, 2, 1)
    blk = lambda e, lag: jnp.clip(e - lag, 0, n_eb - 1)
    stat = pl.BlockSpec((n_heads, n_keys, tt), lambda i, e: (0, 0, i), pipeline_mode=pl.Buffered(1))
    rowstat = pl.BlockSpec((n_heads, i_blk, tt), lambda i, e: (0, blk(e, 1), i))
    kern = functools.partial(_peerb_kernel, n_eb=n_eb)
    return pl.pallas_call(
        kern,
        grid=(t // tt, n_eb + 1),
        in_specs=[pl.BlockSpec((tt, d), lambda i, e: (i, 0)),
                  rowstat, rowstat, stat, stat,
                  pl.BlockSpec((et, d), lambda i, e: (blk(e, 0), 0)),
                  pl.BlockSpec((None, d, et), lambda i, e: (blk(e, 1), 0, 0)),
                  pl.BlockSpec((tt, d), lambda i, e: (i, 0), pipeline_mode=pl.Buffered(1)),
                  pl.BlockSpec((1, 1, d), lambda i, e: (i // per_batch, 0, 0)),
                  pl.BlockSpec((1, d), lambda i, e: (0, 0))],
        out_specs=pl.BlockSpec((tt, d), lambda i, e: (i, 0)),
        out_shape=jax.ShapeDtypeStruct((t, d), F32),
        scratch_shapes=[pltpu.VMEM((d, tt), F32),
                        pltpu.VMEM((et, tt), F32), pltpu.VMEM((et, tt), F32),
                        pltpu.VMEM((et, tt), BF16)],
        compiler_params=_cparams(2),
        name="peerb",
    )(hq, thr, coef, s2, e2, u, v_t, x1, g2, final_g.reshape(1, d))


def kernel(x, c, ctx, c_ctx, w_ada, b_ada, norm1_g, norm2_g, w_in, ssd_conv_w, ssd_conv_b, ssd_dt_bias,
           ssd_A_log, ssd_D, ssd_norm_g, ssd_w_out, sc_conv_w, sc_w_out, w_o, peer_w_q, peer_keys,
           peer_u, peer_v, final_g):
    batch, seq_len, d = x.shape
    ctx_len = ctx.shape[1]
    depth = w_in.shape[0]
    assert depth == 1, "single-layer block"
    heads = ssd_D.shape[-1]
    d_ssd = ssd_norm_g.shape[-1]
    n_xbc = ssd_conv_w.shape[-1]
    gn = (n_xbc - d_ssd) // 2
    groups = gn // SSD_STATE
    d_sc = sc_conv_w.shape[-1]
    assert d_ssd // heads == LANE // 2 and heads % (2 * groups) == 0 and 2 * heads <= LANE
    assert d_sc == d and n_xbc % d == 0 and d_ssd % d == 0 and d % LANE == 0
    assert seq_len % SSD_CHUNK == 0 and ctx_len % SSD_CHUNK == 0

    off_dt = n_xbc
    off_z = off_dt + 2 * heads
    off_sc = off_z + d_ssd
    off_gate = off_sc + 3 * d_sc
    w_in0 = w_in[0]
    w_main = jnp.concatenate([w_in0[:, off_z:off_sc], w_in0[:, :off_dt], w_in0[:, off_sc:]],
                             axis=1).astype(BF16)
    w_main = w_main.reshape(d, -1, d).transpose(1, 0, 2)
    w_dt = jnp.pad(w_in0[:, off_dt:off_z], ((0, 0), (0, LANE - 2 * heads))).astype(BF16)
    dt_bias = jnp.pad(ssd_dt_bias[0].reshape(1, 2 * heads), ((0, 0), (0, LANE - 2 * heads)))
    a_log = jnp.pad(ssd_A_log[0].reshape(1, 2 * heads).astype(F32), ((0, 0), (0, LANE - 2 * heads)))
    xbc_blk = d_ssd // d
    scb_blk = (d_ssd + n_xbc) // d
    gate_blk = scb_blk + 3
    assert off_gate - off_sc == 3 * d

    rows = -(-(batch + 1) // 8) * 8
    cvec = jnp.zeros((rows, d), F32).at[:batch].set(c).at[batch].set(c_ctx)
    mod = _ada(cvec, w_ada[0], b_ada[0])
    chunk = lambda r, i: mod[r, i * d:(i + 1) * d]
    lat = lambda i: mod[:batch, i * d:(i + 1) * d].reshape(batch, 1, d)
    sh1, sc1, g1, sh2, sc2, g2 = (lat(i) for i in range(6))
    sh_c = jnp.broadcast_to(chunk(batch, 0), (batch, 1, d))
    sc_c = jnp.broadcast_to(chunk(batch, 1), (batch, 1, d))

    conv_w = ssd_conv_w[0]
    conv_b = ssd_conv_b[0]
    p_c, dt_c = _inproj(ctx.reshape(batch * ctx_len, d), norm1_g[0], sh_c, sc_c,
                        w_main[xbc_blk:xbc_blk + n_xbc // d], w_dt, ctx_len)
    x3_c = _ssd_conv(p_c, conv_w, conv_b, batch, ctx_len, n_xbc, 0)
    dt_c = dt_c.reshape(batch, ctx_len, LANE)
    zero_state = jnp.zeros((batch, heads // 2, LANE, SSD_STATE), F32)
    _, hf_c = _ssd_scan(x3_c, dt_c, dt_bias, a_log, zero_state, heads, groups, False)
    _, hb_c = _ssd_scan(x3_c, dt_c, dt_bias, a_log, zero_state, heads, groups, True)

    x2d = x.reshape(batch * seq_len, d)
    p, dt_l = _inproj(x2d, norm1_g[0], sh1, sc1, w_main, w_dt, seq_len)
    x3 = _ssd_conv(p, conv_w, conv_b, batch, seq_len, n_xbc, xbc_blk)
    dt_l = dt_l.reshape(batch, seq_len, LANE)
    drow = jnp.repeat(ssd_D[0], LANE // 2).reshape(heads // 2, LANE)
    y_f, _ = _ssd_scan(x3, dt_l, dt_bias, a_log, hf_c, heads, groups, False)
    y, _ = _ssd_scan(x3, dt_l, dt_bias, a_log, hb_c, heads, groups, True, drow=drow, yprev=y_f)

    m1 = _mixa(y.reshape(batch * seq_len, d_ssd), p, ssd_norm_g[0], ssd_w_out[0].astype(BF16), gate_blk)
    x1, hq = _mixb(p, m1, x2d, g1, sc_conv_w[0], sc_w_out[0].astype(BF16), w_o[0].astype(BF16),
                   norm2_g[0], sh2, sc2, seq_len, scb_blk, gate_blk + 1)

    thr, coef, s2, e2 = _peera(hq, peer_w_q[0].T.astype(BF16), peer_keys[0].astype(BF16))
    out = _peerb(hq, thr, coef, s2, e2, peer_u[0].astype(BF16), peer_v[0].astype(BF16), x1, g2,
                 final_g, seq_len)
    return out.reshape(batch, seq_len, d)
```

```python
import functools
import math

import jax
import jax.numpy as jnp
from jax import lax
from jax.experimental import pallas as pl
from jax.experimental.pallas import tpu as pltpu

F32 = jnp.float32
BF16 = jnp.bfloat16

LANE = 128
BF16_ROWS = 16
MXU_WIDTH = 256
VMEM_LIMIT_BYTES = 56 * 1024 * 1024

EPS = 1e-6
SSD_CHUNK = 128
SSD_STATE = 128
GRID_W = 64
PEER_TOPK = 16
NEG = -1e30

HIGHEST = lax.Precision.HIGHEST


def _cparams(n_axes, flags=None):
    return pltpu.CompilerParams(dimension_semantics=("arbitrary",) * n_axes,
                                vmem_limit_bytes=VMEM_LIMIT_BYTES, flags=flags)


def _resident(block_shape, index_map):
    return pl.BlockSpec(block_shape, index_map, pipeline_mode=pl.Buffered(1))


def _silu(v):
    return v * jax.nn.sigmoid(v)


def _rms(v, g):
    return v * lax.rsqrt(jnp.mean(v * v, axis=-1, keepdims=True) + EPS) * g


def _nt_dot(a, b):
    return lax.dot_general(a, b, (((1,), (1,)), ((), ())), preferred_element_type=F32)


def _ada_kernel(c_ref, w_ref, b_ref, o_ref):
    s = _silu(c_ref[...])
    o_ref[...] = jnp.dot(s, w_ref[...], preferred_element_type=F32, precision=HIGHEST) + b_ref[...]


def _ada(cvec, w, b):
    rows, d = cvec.shape
    n = w.shape[1]
    tn = 1024 if n % 1024 == 0 else d
    return pl.pallas_call(
        _ada_kernel,
        grid=(n // tn,),
        in_specs=[pl.BlockSpec((rows, d), lambda j: (0, 0)),
                  pl.BlockSpec((d, tn), lambda j: (0, j)),
                  pl.BlockSpec((1, tn), lambda j: (0, j))],
        out_specs=pl.BlockSpec((rows, tn), lambda j: (0, j)),
        out_shape=jax.ShapeDtypeStruct((rows, n), F32),
        compiler_params=_cparams(1),
        name="ada",
    )(cvec, w, b.reshape(1, n))


def _inproj_kernel(x_ref, g_ref, sh_ref, sc_ref, w_ref, wdt_ref, p_ref, dt_ref, h_scr):
    @pl.when(pl.program_id(1) == 0)
    def _():
        h = _rms(x_ref[...], g_ref[...]) * (1.0 + sc_ref[0]) + sh_ref[0]
        hb = h.astype(BF16)
        h_scr[...] = hb
        dt_ref[...] = jnp.dot(hb, wdt_ref[...], preferred_element_type=F32)

    p_ref[...] = jnp.dot(h_scr[...], w_ref[...], preferred_element_type=F32).astype(p_ref.dtype)


def _inproj(x2d, norm_g, shift, scale, w_main, w_dt, seq_len):
    t, d = x2d.shape
    n_blk, _, tn = w_main.shape
    tm = min(512, seq_len)
    per_batch = seq_len // tm
    return pl.pallas_call(
        _inproj_kernel,
        grid=(t // tm, n_blk),
        in_specs=[pl.BlockSpec((tm, d), lambda i, j: (i, 0)),
                  pl.BlockSpec((1, d), lambda i, j: (0, 0)),
                  pl.BlockSpec((1, 1, d), lambda i, j: (i // per_batch, 0, 0)),
                  pl.BlockSpec((1, 1, d), lambda i, j: (i // per_batch, 0, 0)),
                  pl.BlockSpec((None, d, tn), lambda i, j: (j, 0, 0)),
                  pl.BlockSpec((d, LANE), lambda i, j: (0, 0))],
        out_specs=[pl.BlockSpec((None, tm, tn), lambda i, j: (j, i, 0)),
                   pl.BlockSpec((tm, LANE), lambda i, j: (i, 0))],
        out_shape=[jax.ShapeDtypeStruct((n_blk, t, tn), BF16),
                   jax.ShapeDtypeStruct((t, LANE), F32)],
        scratch_shapes=[pltpu.VMEM((tm, d), BF16)],
        compiler_params=_cparams(2),
        name="inproj",
    )(x2d, norm_g.reshape(1, d), shift, scale, w_main, w_dt)


def _conv_kernel(main_ref, prev_ref, next_ref, w_ref, b_ref, o_ref, e_scr, *, rows, n_rb, taps):
    rb = pl.program_id(1)
    halo = BF16_ROWS
    pad = taps // 2
    e_scr[halo:halo + rows, :] = main_ref[...].astype(F32)
    e_scr[0:halo, :] = jnp.where(rb > 0, prev_ref[...].astype(F32), 0.0)
    e_scr[halo + rows:2 * halo + rows, :] = jnp.where(rb < n_rb - 1, next_ref[...].astype(F32), 0.0)
    acc = b_ref[...] + w_ref[0:1, :] * e_scr[halo - pad:halo - pad + rows, :]
    for k in range(1, taps):
        acc = acc + w_ref[k:k + 1, :] * e_scr[halo - pad + k:halo - pad + k + rows, :]
    act = _silu(acc).astype(o_ref.dtype)
    for t in range(o_ref.shape[1]):
        o_ref[0, t] = act[:, t * LANE:(t + 1) * LANE]


def _ssd_conv(p3, conv_w, conv_b, batch, seq_len, n_xbc, col_blk):
    tc = p3.shape[2]
    taps = conv_w.shape[0]
    rows = min(512, seq_len)
    n_rb = seq_len // rows
    n_col = n_xbc // tc
    halo = BF16_ROWS
    blocks_per_rb = rows // halo
    last_halo_block = batch * seq_len // halo - 1
    kern = functools.partial(_conv_kernel, rows=rows, n_rb=n_rb, taps=taps)
    return pl.pallas_call(
        kern,
        grid=(batch, n_rb, n_col),
        in_specs=[
            pl.BlockSpec((None, rows, tc), lambda b, r, j: (j + col_blk, b * n_rb + r, 0)),
            pl.BlockSpec((None, halo, tc),
                         lambda b, r, j: (j + col_blk, jnp.maximum((b * n_rb + r) * blocks_per_rb - 1, 0), 0)),
            pl.BlockSpec((None, halo, tc),
                         lambda b, r, j: (j + col_blk,
                                          jnp.minimum((b * n_rb + r + 1) * blocks_per_rb, last_halo_block), 0)),
            pl.BlockSpec((taps, tc), lambda b, r, j: (0, j)),
            pl.BlockSpec((1, tc), lambda b, r, j: (0, j)),
        ],
        out_specs=pl.BlockSpec((1, tc // LANE, rows, LANE), lambda b, r, j: (b, j, r, 0)),
        out_shape=jax.ShapeDtypeStruct((batch, n_xbc // LANE, seq_len, LANE), BF16),
        scratch_shapes=[pltpu.VMEM((rows + 2 * halo, tc), F32)],
        compiler_params=_cparams(3),
        name="ssdconv",
    )(p3, p3, p3, conv_w, conv_b.reshape(1, n_xbc))


def _ssd_kernel(*refs, reverse, add_prev, n_chunks, heads, groups):
    if add_prev:
        (x3_ref, dtr_ref, bias_ref, alog_ref, h0_ref, drow_ref, yprev_ref,
         y_ref, hfin_ref, h_scr) = refs
    else:
        x3_ref, dtr_ref, bias_ref, alog_ref, h0_ref, y_ref, hfin_ref, h_scr = refs
    q = SSD_CHUNK
    n_pairs = heads // 2
    pairs_per_group = n_pairs // groups
    col0 = heads if reverse else 0
    step = pl.program_id(1)

    @pl.when(step == 0)
    def _():
        h_scr[...] = h0_ref[0]

    pre = dtr_ref[0] + bias_ref[...]
    dt = jnp.maximum(pre, 0.0) + jnp.log1p(jnp.exp(-jnp.abs(pre)))
    a = dt * (-jnp.exp(alog_ref[...]))
    ri = lax.broadcasted_iota(jnp.int32, (q, q), 0)
    ci = lax.broadcasted_iota(jnp.int32, (q, q), 1)
    tri = (ci >= ri) if reverse else (ci <= ri)
    tri_t = (ri >= ci) if reverse else (ri <= ci)
    acum = jnp.dot(tri.astype(F32), a, preferred_element_type=F32, precision=HIGHEST)
    a_t = a.T
    dt_t = dt.T
    acum_t = jnp.dot(a_t, tri_t.astype(F32), preferred_element_type=F32, precision=HIGHEST)
    tot = jnp.dot(a_t, jnp.ones((q, q), F32), preferred_element_type=F32, precision=HIGHEST)
    w_t = dt_t * jnp.exp(tot - acum_t)
    dec_t = jnp.exp(tot)
    low_lanes = ci < (LANE // 2)
    low_rows = ri < (LANE // 2)

    cb = None
    for kp in range(n_pairs):
        g = kp // pairs_per_group
        b_g = x3_ref[0, n_pairs + g]
        c_g = x3_ref[0, n_pairs + groups + g]
        if kp % pairs_per_group == 0:
            cb = _nt_dot(c_g, b_g)
            c_f = c_g.astype(F32)
        ms, cds = [], []
        for hc in (col0 + 2 * kp, col0 + 2 * kp + 1):
            col = jnp.broadcast_to(acum[:, hc:hc + 1], (q, q))
            seg = col - acum_t[hc:hc + 1, :]
            decay = jnp.exp(jnp.where(tri, seg, NEG))
            ms.append((cb * decay * dt_t[hc:hc + 1, :]).astype(BF16))
            cds.append((c_f * jnp.exp(col)).astype(BF16))
        xp = x3_ref[0, kp].astype(F32)
        rhs = jnp.concatenate([jnp.where(low_lanes, xp, 0.0), jnp.where(low_lanes, 0.0, xp)],
                              axis=0).astype(BF16)
        y = jnp.dot(jnp.concatenate(ms, axis=1), rhs, preferred_element_type=F32)
        hp = h_scr[kp]
        rhs2 = jnp.concatenate([jnp.where(low_rows, hp, 0.0), jnp.where(low_rows, 0.0, hp)],
                               axis=1).astype(BF16)
        y = y + _nt_dot(jnp.concatenate(cds, axis=1), rhs2)
        if add_prev:
            y = y + drow_ref[kp:kp + 1, :] * xp + yprev_ref[0, :, kp * LANE:(kp + 1) * LANE].astype(F32)
        y_ref[0, :, kp * LANE:(kp + 1) * LANE] = y.astype(y_ref.dtype)
        h1, h2 = col0 + 2 * kp, col0 + 2 * kp + 1
        w_sel = jnp.where(low_rows, w_t[h1:h1 + 1, :], w_t[h2:h2 + 1, :])
        d_sel = jnp.where(low_rows, dec_t[h1:h1 + 1, :], dec_t[h2:h2 + 1, :])
        xw_t = (xp.T * w_sel).astype(BF16)
        h_scr[kp] = d_sel * hp + jnp.dot(xw_t, b_g, preferred_element_type=F32)

    @pl.when(step == n_chunks - 1)
    def _():
        hfin_ref[0] = h_scr[...]


def _ssd_scan(x3, dtraw, dt_bias, a_log, h0, heads, groups, reverse, drow=None, yprev=None):
    batch, n_tiles, seq_len, _ = x3.shape
    n_chunks = seq_len // SSD_CHUNK
    n_pairs = heads // 2
    d_ssd = n_pairs * LANE
    add_prev = yprev is not None
    if reverse:
        chunk = lambda k: n_chunks - 1 - k
    else:
        chunk = lambda k: k
    in_specs = [
        pl.BlockSpec((1, n_tiles, SSD_CHUNK, LANE), lambda b, k: (b, 0, chunk(k), 0)),
        pl.BlockSpec((1, SSD_CHUNK, LANE), lambda b, k: (b, chunk(k), 0)),
        pl.BlockSpec((1, LANE), lambda b, k: (0, 0)),
        pl.BlockSpec((1, LANE), lambda b, k: (0, 0)),
        pl.BlockSpec((1, n_pairs, LANE, SSD_STATE), lambda b, k: (b, 0, 0, 0)),
    ]
    args = [x3, dtraw, dt_bias, a_log, h0]
    if add_prev:
        in_specs += [pl.BlockSpec((n_pairs, LANE), lambda b, k: (0, 0)),
                     pl.BlockSpec((1, SSD_CHUNK, d_ssd), lambda b, k: (b, chunk(k), 0))]
        args += [drow, yprev]
    kern = functools.partial(_ssd_kernel, reverse=reverse, add_prev=add_prev, n_chunks=n_chunks,
                             heads=heads, groups=groups)
    return pl.pallas_call(
        kern,
        grid=(batch, n_chunks),
        in_specs=in_specs,
        out_specs=[pl.BlockSpec((1, SSD_CHUNK, d_ssd), lambda b, k: (b, chunk(k), 0)),
                   pl.BlockSpec((1, n_pairs, LANE, SSD_STATE), lambda b, k: (b, 0, 0, 0))],
        out_shape=[jax.ShapeDtypeStruct((batch, seq_len, d_ssd), BF16),
                   jax.ShapeDtypeStruct((batch, n_pairs, LANE, SSD_STATE), F32)],
        scratch_shapes=[pltpu.VMEM((n_pairs, LANE, SSD_STATE), F32)],
        compiler_params=_cparams(2),
        name="ssd_bwd" if reverse else "ssd_fwd",
    )(*args)


def _mixa_kernel(*refs):
    y_ref, *z_refs, gate_ref, ng_ref, w_ref, o_ref = refs
    z = jnp.concatenate([r[...] for r in z_refs], axis=1).astype(F32)
    v = _rms(y_ref[...].astype(F32) * _silu(z), ng_ref[...])
    o = jnp.dot(v.astype(BF16), w_ref[...], preferred_element_type=F32)
    o_ref[...] = (jax.nn.sigmoid(gate_ref[...].astype(F32)) * o).astype(o_ref.dtype)


def _mixa(y2d, p3, norm_g, w_out, gate_blk):
    t, d_ssd = y2d.shape
    d = w_out.shape[1]
    tm = min(256, t)
    n_z = d_ssd // d
    return pl.pallas_call(
        _mixa_kernel,
        grid=(t // tm,),
        in_specs=[pl.BlockSpec((tm, d_ssd), lambda i: (i, 0))]
                 + [pl.BlockSpec((None, tm, d), functools.partial(lambda k, i: (k, i, 0), k))
                    for k in range(n_z)]
                 + [pl.BlockSpec((None, tm, d), lambda i: (gate_blk, i, 0)),
                  pl.BlockSpec((1, d_ssd), lambda i: (0, 0)),
                  _resident((d_ssd, d), lambda i: (0, 0))],
        out_specs=pl.BlockSpec((tm, d), lambda i: (i, 0)),
        out_shape=jax.ShapeDtypeStruct((t, d), BF16),
        compiler_params=_cparams(1),
        name="mixa",
    )(y2d, *([p3] * (n_z + 1)), norm_g.reshape(1, d_ssd), w_out)


def _mixb_kernel(scb_ref, scc_ref, scx_ref, gsc_ref, m1_ref, x_ref, g1_ref, cw_ref, wsc_ref, wo_ref,
                 n2_ref, sh2_ref, sc2_ref, x1_ref, hq_ref, u_scr, *, rows):
    pad = 8
    u_scr[0:pad, :] = jnp.zeros((pad, u_scr.shape[1]), F32)
    u_scr[pad + rows:2 * pad + rows, :] = jnp.zeros((pad, u_scr.shape[1]), F32)
    u_scr[pad:pad + rows, :] = scc_ref[...].astype(F32) * scx_ref[...].astype(F32)
    pos = jnp.bitwise_and(lax.broadcasted_iota(jnp.int32, (rows, u_scr.shape[1]), 0), GRID_W - 1)
    left = jnp.where(pos == 0, 0.0, u_scr[pad - 1:pad - 1 + rows, :])
    right = jnp.where(pos == GRID_W - 1, 0.0, u_scr[pad + 1:pad + 1 + rows, :])
    conv = cw_ref[0:1, :] * left + cw_ref[1:2, :] * u_scr[pad:pad + rows, :] + cw_ref[2:3, :] * right
    v = (scb_ref[...].astype(F32) * conv).astype(BF16)
    y_sc = jnp.dot(v, wsc_ref[...], preferred_element_type=F32)
    merged = m1_ref[...].astype(F32) + jax.nn.sigmoid(gsc_ref[...].astype(F32)) * y_sc
    out = jnp.dot(merged.astype(BF16), wo_ref[...], preferred_element_type=F32)
    x1 = x_ref[...] + g1_ref[0] * out
    x1_ref[...] = x1
    hq_ref[...] = (_rms(x1, n2_ref[...]) * (1.0 + sc2_ref[0]) + sh2_ref[0]).astype(hq_ref.dtype)


def _mixb(p3, m1, x2d, g1, sc_conv_w, w_sc, w_o, norm2_g, sh2, sc2, seq_len, scb_blk, gsc_blk):
    t, d = x2d.shape
    tm = min(256, t)
    per_batch = seq_len // tm
    bmap = lambda i: (i // per_batch, 0, 0)
    tile = lambda blk: pl.BlockSpec((tm, d), lambda i: (i, blk))
    ptile = lambda blk: pl.BlockSpec((None, tm, d), lambda i: (blk, i, 0))
    kern = functools.partial(_mixb_kernel, rows=tm)
    return pl.pallas_call(
        kern,
        grid=(t // tm,),
        in_specs=[ptile(scb_blk), ptile(scb_blk + 1), ptile(scb_blk + 2), ptile(gsc_blk),
                  tile(0), tile(0),
                  pl.BlockSpec((1, 1, d), bmap),
                  pl.BlockSpec(sc_conv_w.shape, lambda i: (0, 0)),
                  _resident((d, d), lambda i: (0, 0)),
                  _resident((d, d), lambda i: (0, 0)),
                  pl.BlockSpec((1, d), lambda i: (0, 0)),
                  pl.BlockSpec((1, 1, d), bmap),
                  pl.BlockSpec((1, 1, d), bmap)],
        out_specs=[tile(0), tile(0)],
        out_shape=[jax.ShapeDtypeStruct((t, d), F32), jax.ShapeDtypeStruct((t, d), BF16)],
        scratch_shapes=[pltpu.VMEM((tm + 16, d), F32)],
        compiler_params=_cparams(1),
        name="mixb",
    )(p3, p3, p3, p3, m1, x2d, g1, sc_conv_w, w_sc, w_o, norm2_g.reshape(1, d), sh2, sc2)


def _top_values(cur, n, sv_ref):
    for it in range(n):
        m = jnp.max(cur, axis=0, keepdims=True)
        sv_ref[it:it + 1, :] = m
        cur = jnp.where(cur >= m, NEG, cur)


def _peera_kernel(hq_ref, wq_ref, keys_ref, thr_ref, coef_ref, s2_ref, e2_ref,
                  sv1_scr, sv2_scr, cand_scr, svc_scr):
    k = PEER_TOPK
    sub = 8
    n_heads = keys_ref.shape[0]
    dk = keys_ref.shape[3]
    q_t = _nt_dot(wq_ref[...], hq_ref[...])
    row = lax.broadcasted_iota(jnp.int32, (sub, q_t.shape[1]), 0)
    for h in range(n_heads):
        s_t = []
        for s in range(2):
            r0 = (2 * h + s) * dk
            s_t.append(jnp.dot(keys_ref[h, s], q_t[r0:r0 + dk, :].astype(BF16),
                               preferred_element_type=F32))
        _top_values(s_t[0], k + 1, sv1_scr)
        _top_values(s_t[1], k + 1, sv2_scr)
        cand_scr[0:k, :] = sv1_scr[0:k, :] + sv2_scr[0:1, :]
        for b in range(1, sub):
            blk = sv1_scr[0:sub, :] + sv2_scr[b:b + 1, :]
            cand_scr[k + (b - 1) * sub:k + b * sub, :] = jnp.where(row < k // (b + 1), blk, NEG)
        cand_scr[k + (sub - 1) * sub:k + sub * sub, :] = sv2_scr[sub:k, :] + sv1_scr[0:1, :]
        cand = cand_scr[...]
        _top_values(cand, k + 1, svc_scr)
        c_next = jnp.maximum(svc_scr[k:k + 1, :],
                             jnp.maximum(sv1_scr[k:k + 1, :] + sv2_scr[0:1, :],
                                         sv1_scr[0:1, :] + sv2_scr[k:k + 1, :]))
        tau = 0.5 * (svc_scr[k - 1:k, :] + c_next)
        top = sv1_scr[0:1, :] + sv2_scr[0:1, :]
        z = jnp.sum(jnp.where(cand > tau, jnp.exp(cand - top), 0.0), axis=0, keepdims=True)
        thr_ref[h] = tau - s_t[0]
        coef_ref[h] = jnp.exp(s_t[0] - sv1_scr[0:1, :]) / z
        s2_ref[h] = s_t[1]
        e2_ref[h] = jnp.exp(s_t[1] - sv2_scr[0:1, :])


def _peera(hq, wq_t, keys):
    t, d = hq.shape
    n_heads, _, n_keys, _ = keys.shape
    tt = min(512, t)
    k = PEER_TOPK
    n_cand = k + 64
    out = jax.ShapeDtypeStruct((n_heads, n_keys, t), F32)
    ospec = pl.BlockSpec((n_heads, n_keys, tt), lambda i: (0, 0, i))
    return pl.pallas_call(
        _peera_kernel,
        grid=(t // tt,),
        in_specs=[pl.BlockSpec((tt, d), lambda i: (i, 0)),
                  _resident(wq_t.shape, lambda i: (0, 0)),
                  pl.BlockSpec(keys.shape, lambda i: (0, 0, 0, 0))],
        out_specs=[ospec] * 4,
        out_shape=[out] * 4,
        scratch_shapes=[pltpu.VMEM((k + 8, tt), F32), pltpu.VMEM((k + 8, tt), F32),
                        pltpu.VMEM((n_cand, tt), F32), pltpu.VMEM((k + 8, tt), F32)],
        compiler_params=_cparams(1),
        name="peera",
    )(hq, wq_t, keys)


def _peerb_kernel(hq_ref, thr_ref, coef_ref, s2_ref, e2_ref, u_ref, vt_ref, x1_ref, g2_ref, fg_ref,
                  o_ref, acc_scr, a0_scr, a1_scr, wg_new, *, n_eb):
    s = pl.program_id(0)
    blk_v = lax.rem(jnp.maximum(s - 1, 0), n_eb)
    n_heads, n_keys, _ = s2_ref.shape

    @pl.when(s == 0)
    def _():
        a1_scr[...] = jnp.zeros_like(a1_scr)

    @pl.when(blk_v == 0)
    def _():
        acc_scr[...] = jnp.zeros_like(acc_scr)

    def stages(a_new, a_old):
        half = jnp.where(s >= 1, 0.5, 0.0)
        n_i = thr_ref.shape[1]

        def gated_weights(sub):
            lanes = slice(sub * LANE, (sub + 1) * LANE)
            n_jc = n_keys // BF16_ROWS
            for ii in range(n_i):
                w = [None] * n_jc
                for h in range(n_heads):
                    thr = jnp.broadcast_to(thr_ref[h, ii:ii + 1, lanes], (BF16_ROWS, LANE))
                    coef = jnp.broadcast_to(coef_ref[h, ii:ii + 1, lanes], (BF16_ROWS, LANE))
                    for jc in range(n_jc):
                        jrows = slice(jc * BF16_ROWS, (jc + 1) * BF16_ROWS)
                        term = coef * jnp.where(s2_ref[h, jrows, lanes] >= thr, e2_ref[h, jrows, lanes], 0.0)
                        w[jc] = term if w[jc] is None else w[jc] + term
                for jc in range(n_jc):
                    rows = slice(ii * n_keys + jc * BF16_ROWS, ii * n_keys + (jc + 1) * BF16_ROWS)
                    act = a_old[rows, lanes]
                    gelu = (half * act) * (1.0 + lax.erf(act * (1.0 / math.sqrt(2.0))))
                    wg_new[rows, lanes] = (w[jc] * gelu).astype(BF16)

        tt = s2_ref.shape[2]
        piece = min(MXU_WIDTH, tt)
        subs = piece // LANE
        for pc in range(tt // piece):
            cols = slice(pc * piece, (pc + 1) * piece)
            for sub in range(pc * subs, pc * subs + (subs + 1) // 2):
                gated_weights(sub)
            a_new[:, cols] = _nt_dot(u_ref[...], hq_ref[cols, :])
            for sub in range(pc * subs + (subs + 1) // 2, (pc + 1) * subs):
                gated_weights(sub)
            acc_scr[:, cols] += jnp.dot(vt_ref[...], wg_new[:, cols], preferred_element_type=F32)

    @pl.when(s % 2 == 0)
    def _():
        stages(a0_scr, a1_scr)

    @pl.when(s % 2 == 1)
    def _():
        stages(a1_scr, a0_scr)

    @pl.when((s >= 1) & (blk_v == n_eb - 1))
    def _():
        x2 = x1_ref[...] + g2_ref[0] * acc_scr[...].T
        o_ref[...] = _rms(x2, fg_ref[...])


def _peerb(hq, thr, coef, s2, e2, u, v, x1, g2, final_g, seq_len):
    t, d = hq.shape
    n_heads, n_keys, _ = s2.shape
    n_exp = u.shape[0]
    tt = min(512, t)
    i_blk = 8
    et = i_blk * n_keys
    per_batch = seq_len // tt
    n_eb = n_exp // et
    v_t = v.reshape(n_eb, et, d).transpose(0, 2, 1)
    n_tiles = t // tt
    tile_a = lambda s: jnp.minimum(s // n_eb, n_tiles - 1)
    blk_a = lambda s: s % n_eb
    tile_v = lambda s: jnp.maximum(s - 1, 0) // n_eb
    blk_v = lambda s: jnp.maximum(s - 1, 0) % n_eb
    stat = pl.BlockSpec((n_heads, n_keys, tt), lambda s: (0, 0, tile_v(s)))
    rowstat = pl.BlockSpec((n_heads, i_blk, tt), lambda s: (0, blk_v(s), tile_v(s)))
    kern = functools.partial(_peerb_kernel, n_eb=n_eb)
    return pl.pallas_call(
        kern,
        grid=(n_tiles * n_eb + 1,),
        in_specs=[pl.BlockSpec((tt, d), lambda s: (tile_a(s), 0)),
                  rowstat, rowstat, stat, stat,
                  pl.BlockSpec((et, d), lambda s: (blk_a(s), 0)),
                  pl.BlockSpec((None, d, et), lambda s: (blk_v(s), 0, 0)),
                  pl.BlockSpec((tt, d), lambda s: (tile_v(s), 0), pipeline_mode=pl.Buffered(1)),
                  pl.BlockSpec((1, 1, d), lambda s: (tile_v(s) // per_batch, 0, 0)),
                  pl.BlockSpec((1, d), lambda s: (0, 0))],
        out_specs=pl.BlockSpec((tt, d), lambda s: (tile_v(s), 0)),
        out_shape=jax.ShapeDtypeStruct((t, d), F32),
        scratch_shapes=[pltpu.VMEM((d, tt), F32),
                        pltpu.VMEM((et, tt), F32), pltpu.VMEM((et, tt), F32),
                        pltpu.VMEM((et, tt), BF16)],
        compiler_params=_cparams(1),
        name="peerb",
    )(hq, thr, coef, s2, e2, u, v_t, x1, g2, final_g.reshape(1, d))


def kernel(x, c, ctx, c_ctx, w_ada, b_ada, norm1_g, norm2_g, w_in, ssd_conv_w, ssd_conv_b, ssd_dt_bias,
           ssd_A_log, ssd_D, ssd_norm_g, ssd_w_out, sc_conv_w, sc_w_out, w_o, peer_w_q, peer_keys,
           peer_u, peer_v, final_g):
    batch, seq_len, d = x.shape
    ctx_len = ctx.shape[1]
    depth = w_in.shape[0]
    assert depth == 1, "single-layer block"
    heads = ssd_D.shape[-1]
    d_ssd = ssd_norm_g.shape[-1]
    n_xbc = ssd_conv_w.shape[-1]
    gn = (n_xbc - d_ssd) // 2
    groups = gn // SSD_STATE
    d_sc = sc_conv_w.shape[-1]
    assert d_ssd // heads == LANE // 2 and heads % (2 * groups) == 0 and 2 * heads <= LANE
    assert d_sc == d and n_xbc % d == 0 and d_ssd % d == 0 and d % LANE == 0
    assert seq_len % SSD_CHUNK == 0 and ctx_len % SSD_CHUNK == 0

    off_dt = n_xbc
    off_z = off_dt + 2 * heads
    off_sc = off_z + d_ssd
    off_gate = off_sc + 3 * d_sc
    w_in0 = w_in[0]
    w_main = jnp.concatenate([w_in0[:, off_z:off_sc], w_in0[:, :off_dt], w_in0[:, off_sc:]],
                             axis=1).astype(BF16)
    w_main = w_main.reshape(d, -1, d).transpose(1, 0, 2)
    w_dt = jnp.pad(w_in0[:, off_dt:off_z], ((0, 0), (0, LANE - 2 * heads))).astype(BF16)
    dt_bias = jnp.pad(ssd_dt_bias[0].reshape(1, 2 * heads), ((0, 0), (0, LANE - 2 * heads)))
    a_log = jnp.pad(ssd_A_log[0].reshape(1, 2 * heads).astype(F32), ((0, 0), (0, LANE - 2 * heads)))
    xbc_blk = d_ssd // d
    scb_blk = (d_ssd + n_xbc) // d
    gate_blk = scb_blk + 3
    assert off_gate - off_sc == 3 * d

    rows = -(-(batch + 1) // 8) * 8
    cvec = jnp.zeros((rows, d), F32).at[:batch].set(c).at[batch].set(c_ctx)
    mod = _ada(cvec, w_ada[0], b_ada[0])
    chunk = lambda r, i: mod[r, i * d:(i + 1) * d]
    lat = lambda i: mod[:batch, i * d:(i + 1) * d].reshape(batch, 1, d)
    sh1, sc1, g1, sh2, sc2, g2 = (lat(i) for i in range(6))
    sh_c = jnp.broadcast_to(chunk(batch, 0), (batch, 1, d))
    sc_c = jnp.broadcast_to(chunk(batch, 1), (batch, 1, d))

    conv_w = ssd_conv_w[0]
    conv_b = ssd_conv_b[0]
    p_c, dt_c = _inproj(ctx.reshape(batch * ctx_len, d), norm1_g[0], sh_c, sc_c,
                        w_main[xbc_blk:xbc_blk + n_xbc // d], w_dt, ctx_len)
    x3_c = _ssd_conv(p_c, conv_w, conv_b, batch, ctx_len, n_xbc, 0)
    dt_c = dt_c.reshape(batch, ctx_len, LANE)
    zero_state = jnp.zeros((batch, heads // 2, LANE, SSD_STATE), F32)
    _, hf_c = _ssd_scan(x3_c, dt_c, dt_bias, a_log, zero_state, heads, groups, False)
    _, hb_c = _ssd_scan(x3_c, dt_c, dt_bias, a_log, zero_state, heads, groups, True)

    x2d = x.reshape(batch * seq_len, d)
    p, dt_l = _inproj(x2d, norm1_g[0], sh1, sc1, w_main, w_dt, seq_len)
    x3 = _ssd_conv(p, conv_w, conv_b, batch, seq_len, n_xbc, xbc_blk)
    dt_l = dt_l.reshape(batch, seq_len, LANE)
    drow = jnp.repeat(ssd_D[0], LANE // 2).reshape(heads // 2, LANE)
    y_f, _ = _ssd_scan(x3, dt_l, dt_bias, a_log, hf_c, heads, groups, False)
    y, _ = _ssd_scan(x3, dt_l, dt_bias, a_log, hb_c, heads, groups, True, drow=drow, yprev=y_f)

    m1 = _mixa(y.reshape(batch * seq_len, d_ssd), p, ssd_norm_g[0], ssd_w_out[0].astype(BF16), gate_blk)
    x1, hq = _mixb(p, m1, x2d, g1, sc_conv_w[0], sc_w_out[0].astype(BF16), w_o[0].astype(BF16),
                   norm2_g[0], sh2, sc2, seq_len, scb_blk, gate_blk + 1)

    thr, coef, s2, e2 = _peera(hq, peer_w_q[0].T.astype(BF16), peer_keys[0].astype(BF16))
    out = _peerb(hq, thr, coef, s2, e2, peer_u[0].astype(BF16), peer_v[0].astype(BF16), x1, g2,
                 final_g, seq_len)
    return out.reshape(batch, seq_len, d)
```

```python
import functools
import math

import jax
import jax.numpy as jnp
from jax import lax
from jax.experimental import pallas as pl
from jax.experimental.pallas import tpu as pltpu

F32 = jnp.float32
BF16 = jnp.bfloat16

LANE = 128
BF16_ROWS = 16
MXU_WIDTH = 256
VMEM_LIMIT_BYTES = 56 * 1024 * 1024

EPS = 1e-6
SSD_CHUNK = 128
SSD_STATE = 128
GRID_W = 64
PEER_TOPK = 16
NEG = -1e30

HIGHEST = lax.Precision.HIGHEST


def _cparams(n_axes, flags=None):
    return pltpu.CompilerParams(dimension_semantics=("arbitrary",) * n_axes,
                                vmem_limit_bytes=VMEM_LIMIT_BYTES, flags=flags)


def _resident(block_shape, index_map):
    return pl.BlockSpec(block_shape, index_map, pipeline_mode=pl.Buffered(1))


def _silu(v):
    return v * jax.nn.sigmoid(v)


def _rms(v, g):
    return v * lax.rsqrt(jnp.mean(v * v, axis=-1, keepdims=True) + EPS) * g


def _nt_dot(a, b):
    return lax.dot_general(a, b, (((1,), (1,)), ((), ())), preferred_element_type=F32)


def _ada_kernel(c_ref, w_ref, b_ref, o_ref):
    s = _silu(c_ref[...])
    o_ref[...] = jnp.dot(s, w_ref[...], preferred_element_type=F32, precision=HIGHEST) + b_ref[...]


def _ada(cvec, w, b):
    rows, d = cvec.shape
    n = w.shape[1]
    tn = 1024 if n % 1024 == 0 else d
    return pl.pallas_call(
        _ada_kernel,
        grid=(n // tn,),
        in_specs=[pl.BlockSpec((rows, d), lambda j: (0, 0)),
                  pl.BlockSpec((d, tn), lambda j: (0, j)),
                  pl.BlockSpec((1, tn), lambda j: (0, j))],
        out_specs=pl.BlockSpec((rows, tn), lambda j: (0, j)),
        out_shape=jax.ShapeDtypeStruct((rows, n), F32),
        compiler_params=_cparams(1),
        name="ada",
    )(cvec, w, b.reshape(1, n))


def _inproj_kernel(x_ref, g_ref, sh_ref, sc_ref, w_ref, wdt_ref, p_ref, dt_ref, h_scr):
    @pl.when(pl.program_id(1) == 0)
    def _():
        h = _rms(x_ref[...], g_ref[...]) * (1.0 + sc_ref[0]) + sh_ref[0]
        hb = h.astype(BF16)
        h_scr[...] = hb
        dt_ref[...] = jnp.dot(hb, wdt_ref[...], preferred_element_type=F32)

    p_ref[...] = jnp.dot(h_scr[...], w_ref[...], preferred_element_type=F32).astype(p_ref.dtype)


def _inproj(x2d, norm_g, shift, scale, w_main, w_dt, seq_len):
    t, d = x2d.shape
    n_blk, _, tn = w_main.shape
    tm = min(512, seq_len)
    per_batch = seq_len // tm
    return pl.pallas_call(
        _inproj_kernel,
        grid=(t // tm, n_blk),
        in_specs=[pl.BlockSpec((tm, d), lambda i, j: (i, 0)),
                  pl.BlockSpec((1, d), lambda i, j: (0, 0)),
                  pl.BlockSpec((1, 1, d), lambda i, j: (i // per_batch, 0, 0)),
                  pl.BlockSpec((1, 1, d), lambda i, j: (i // per_batch, 0, 0)),
                  pl.BlockSpec((None, d, tn), lambda i, j: (j, 0, 0)),
                  pl.BlockSpec((d, LANE), lambda i, j: (0, 0))],
        out_specs=[pl.BlockSpec((None, tm, tn), lambda i, j: (j, i, 0)),
                   pl.BlockSpec((tm, LANE), lambda i, j: (i, 0))],
        out_shape=[jax.ShapeDtypeStruct((n_blk, t, tn), BF16),
                   jax.ShapeDtypeStruct((t, LANE), F32)],
        scratch_shapes=[pltpu.VMEM((tm, d), BF16)],
        compiler_params=_cparams(2),
        name="inproj",
    )(x2d, norm_g.reshape(1, d), shift, scale, w_main, w_dt)


def _conv_kernel(main_ref, prev_ref, next_ref, w_ref, b_ref, o_ref, e_scr, st_scr, *, rows, n_rb, taps):
    rb = pl.program_id(1)
    halo = BF16_ROWS
    pad = taps // 2
    half = rows // 2
    for t in range(o_ref.shape[1]):
        lanes = slice(t * LANE, (t + 1) * LANE)
        e_scr[t, halo:halo + rows, :] = main_ref[:, lanes].astype(F32)
        e_scr[t, 0:halo, :] = jnp.where(rb > 0, prev_ref[:, lanes].astype(F32), 0.0)
        e_scr[t, halo + rows:2 * halo + rows, :] = jnp.where(rb < n_rb - 1,
                                                             next_ref[:, lanes].astype(F32), 0.0)
    for t in range(o_ref.shape[1]):
        lanes = slice(t * LANE, (t + 1) * LANE)
        for par in range(2):
            acc = b_ref[:, lanes]
            for k in range(taps):
                acc = acc + w_ref[k:k + 1, lanes] * e_scr[t, pl.ds(halo - pad + par + k, half, stride=2), :]
            st_scr[t, pl.ds(par, half, stride=2), :] = _silu(acc)
        o_ref[0, t] = st_scr[t].astype(o_ref.dtype)


def _ssd_conv(p3, conv_w, conv_b, batch, seq_len, n_xbc, col_blk):
    tc = p3.shape[2]
    taps = conv_w.shape[0]
    rows = min(512, seq_len)
    n_rb = seq_len // rows
    n_col = n_xbc // tc
    halo = BF16_ROWS
    blocks_per_rb = rows // halo
    last_halo_block = batch * seq_len // halo - 1
    kern = functools.partial(_conv_kernel, rows=rows, n_rb=n_rb, taps=taps)
    return pl.pallas_call(
        kern,
        grid=(batch, n_rb, n_col),
        in_specs=[
            pl.BlockSpec((None, rows, tc), lambda b, r, j: (j + col_blk, b * n_rb + r, 0)),
            pl.BlockSpec((None, halo, tc),
                         lambda b, r, j: (j + col_blk, jnp.maximum((b * n_rb + r) * blocks_per_rb - 1, 0), 0)),
            pl.BlockSpec((None, halo, tc),
                         lambda b, r, j: (j + col_blk,
                                          jnp.minimum((b * n_rb + r + 1) * blocks_per_rb, last_halo_block), 0)),
            pl.BlockSpec((taps, tc), lambda b, r, j: (0, j)),
            pl.BlockSpec((1, tc), lambda b, r, j: (0, j)),
        ],
        out_specs=pl.BlockSpec((1, tc // LANE, rows, LANE), lambda b, r, j: (b, j, r, 0)),
        out_shape=jax.ShapeDtypeStruct((batch, n_xbc // LANE, seq_len, LANE), BF16),
        scratch_shapes=[pltpu.VMEM((tc // LANE, rows + 2 * halo, LANE), F32),
                        pltpu.VMEM((tc // LANE, rows, LANE), F32)],
        compiler_params=_cparams(3),
        name="ssdconv",
    )(p3, p3, p3, conv_w, conv_b.reshape(1, n_xbc))


def _ssd_kernel(*refs, reverse, add_prev, n_chunks, heads, groups):
    if add_prev:
        (x3_ref, dtr_ref, bias_ref, alog_ref, h0_ref, drow_ref, yprev_ref,
         y_ref, hfin_ref, h_scr) = refs
    else:
        x3_ref, dtr_ref, bias_ref, alog_ref, h0_ref, y_ref, hfin_ref, h_scr = refs
    q = SSD_CHUNK
    n_pairs = heads // 2
    pairs_per_group = n_pairs // groups
    col0 = heads if reverse else 0
    step = pl.program_id(1)

    @pl.when(step == 0)
    def _():
        h_scr[...] = h0_ref[0]

    pre = dtr_ref[0] + bias_ref[...]
    dt = jnp.maximum(pre, 0.0) + jnp.log1p(jnp.exp(-jnp.abs(pre)))
    a = dt * (-jnp.exp(alog_ref[...]))
    ri = lax.broadcasted_iota(jnp.int32, (q, q), 0)
    ci = lax.broadcasted_iota(jnp.int32, (q, q), 1)
    tri = (ci >= ri) if reverse else (ci <= ri)
    tri_t = (ri >= ci) if reverse else (ri <= ci)
    acum = jnp.dot(tri.astype(F32), a, preferred_element_type=F32, precision=HIGHEST)
    a_t = a.T
    dt_t = dt.T
    acum_t = jnp.dot(a_t, tri_t.astype(F32), preferred_element_type=F32, precision=HIGHEST)
    tot = jnp.dot(a_t, jnp.ones((q, q), F32), preferred_element_type=F32, precision=HIGHEST)
    w_t = dt_t * jnp.exp(tot - acum_t)
    dec_t = jnp.exp(tot)
    low_lanes = ci < (LANE // 2)
    low_rows = ri < (LANE // 2)

    cb = None
    for kp in range(n_pairs):
        g = kp // pairs_per_group
        b_g = x3_ref[0, n_pairs + g]
        c_g = x3_ref[0, n_pairs + groups + g]
        if kp % pairs_per_group == 0:
            cb = _nt_dot(c_g, b_g)
            c_f = c_g.astype(F32)
        ms, cds = [], []
        for hc in (col0 + 2 * kp, col0 + 2 * kp + 1):
            col = jnp.broadcast_to(acum[:, hc:hc + 1], (q, q))
            seg = col - acum_t[hc:hc + 1, :]
            decay = jnp.exp(jnp.where(tri, seg, NEG))
            ms.append((cb * decay * dt_t[hc:hc + 1, :]).astype(BF16))
            cds.append((c_f * jnp.exp(col)).astype(BF16))
        xp = x3_ref[0, kp].astype(F32)
        rhs = jnp.concatenate([jnp.where(low_lanes, xp, 0.0), jnp.where(low_lanes, 0.0, xp)],
                              axis=0).astype(BF16)
        y = jnp.dot(jnp.concatenate(ms, axis=1), rhs, preferred_element_type=F32)
        hp = h_scr[kp]
        rhs2 = jnp.concatenate([jnp.where(low_rows, hp, 0.0), jnp.where(low_rows, 0.0, hp)],
                               axis=1).astype(BF16)
        y = y + _nt_dot(jnp.concatenate(cds, axis=1), rhs2)
        if add_prev:
            y = y + drow_ref[kp:kp + 1, :] * xp + yprev_ref[0, :, kp * LANE:(kp + 1) * LANE].astype(F32)
        y_ref[0, :, kp * LANE:(kp + 1) * LANE] = y.astype(y_ref.dtype)
        h1, h2 = col0 + 2 * kp, col0 + 2 * kp + 1
        w_sel = jnp.where(low_rows, w_t[h1:h1 + 1, :], w_t[h2:h2 + 1, :])
        d_sel = jnp.where(low_rows, dec_t[h1:h1 + 1, :], dec_t[h2:h2 + 1, :])
        xw_t = (xp.T * w_sel).astype(BF16)
        h_scr[kp] = d_sel * hp + jnp.dot(xw_t, b_g, preferred_element_type=F32)

    @pl.when(step == n_chunks - 1)
    def _():
        hfin_ref[0] = h_scr[...]


def _ssd_scan(x3, dtraw, dt_bias, a_log, h0, heads, groups, reverse, drow=None, yprev=None):
    batch, n_tiles, seq_len, _ = x3.shape
    n_chunks = seq_len // SSD_CHUNK
    n_pairs = heads // 2
    d_ssd = n_pairs * LANE
    add_prev = yprev is not None
    if reverse:
        chunk = lambda k: n_chunks - 1 - k
    else:
        chunk = lambda k: k
    in_specs = [
        pl.BlockSpec((1, n_tiles, SSD_CHUNK, LANE), lambda b, k: (b, 0, chunk(k), 0)),
        pl.BlockSpec((1, SSD_CHUNK, LANE), lambda b, k: (b, chunk(k), 0)),
        pl.BlockSpec((1, LANE), lambda b, k: (0, 0)),
        pl.BlockSpec((1, LANE), lambda b, k: (0, 0)),
        pl.BlockSpec((1, n_pairs, LANE, SSD_STATE), lambda b, k: (b, 0, 0, 0)),
    ]
    args = [x3, dtraw, dt_bias, a_log, h0]
    if add_prev:
        in_specs += [pl.BlockSpec((n_pairs, LANE), lambda b, k: (0, 0)),
                     pl.BlockSpec((1, SSD_CHUNK, d_ssd), lambda b, k: (b, chunk(k), 0))]
        args += [drow, yprev]
    kern = functools.partial(_ssd_kernel, reverse=reverse, add_prev=add_prev, n_chunks=n_chunks,
                             heads=heads, groups=groups)
    return pl.pallas_call(
        kern,
        grid=(batch, n_chunks),
        in_specs=in_specs,
        out_specs=[pl.BlockSpec((1, SSD_CHUNK, d_ssd), lambda b, k: (b, chunk(k), 0)),
                   pl.BlockSpec((1, n_pairs, LANE, SSD_STATE), lambda b, k: (b, 0, 0, 0))],
        out_shape=[jax.ShapeDtypeStruct((batch, seq_len, d_ssd), BF16),
                   jax.ShapeDtypeStruct((batch, n_pairs, LANE, SSD_STATE), F32)],
        scratch_shapes=[pltpu.VMEM((n_pairs, LANE, SSD_STATE), F32)],
        compiler_params=_cparams(2),
        name="ssd_bwd" if reverse else "ssd_fwd",
    )(*args)


def _mixa_kernel(*refs):
    y_ref, *z_refs, gate_ref, ng_ref, w_ref, o_ref = refs
    z = jnp.concatenate([r[...] for r in z_refs], axis=1).astype(F32)
    v = _rms(y_ref[...].astype(F32) * _silu(z), ng_ref[...])
    o = jnp.dot(v.astype(BF16), w_ref[...], preferred_element_type=F32)
    o_ref[...] = (jax.nn.sigmoid(gate_ref[...].astype(F32)) * o).astype(o_ref.dtype)


def _mixa(y2d, p3, norm_g, w_out, gate_blk):
    t, d_ssd = y2d.shape
    d = w_out.shape[1]
    tm = min(256, t)
    n_z = d_ssd // d
    return pl.pallas_call(
        _mixa_kernel,
        grid=(t // tm,),
        in_specs=[pl.BlockSpec((tm, d_ssd), lambda i: (i, 0))]
                 + [pl.BlockSpec((None, tm, d), functools.partial(lambda k, i: (k, i, 0), k))
                    for k in range(n_z)]
                 + [pl.BlockSpec((None, tm, d), lambda i: (gate_blk, i, 0)),
                  pl.BlockSpec((1, d_ssd), lambda i: (0, 0)),
                  _resident((d_ssd, d), lambda i: (0, 0))],
        out_specs=pl.BlockSpec((tm, d), lambda i: (i, 0)),
        out_shape=jax.ShapeDtypeStruct((t, d), BF16),
        compiler_params=_cparams(1),
        name="mixa",
    )(y2d, *([p3] * (n_z + 1)), norm_g.reshape(1, d_ssd), w_out)


def _mixb_kernel(scb_ref, scc_ref, scx_ref, gsc_ref, m1_ref, x_ref, g1_ref, cw_ref, wsc_ref, wo_ref,
                 n2_ref, sh2_ref, sc2_ref, x1_ref, hq_ref, u_scr, *, rows):
    pad = 8
    u_scr[0:pad, :] = jnp.zeros((pad, u_scr.shape[1]), F32)
    u_scr[pad + rows:2 * pad + rows, :] = jnp.zeros((pad, u_scr.shape[1]), F32)
    u_scr[pad:pad + rows, :] = scc_ref[...].astype(F32) * scx_ref[...].astype(F32)
    pos = jnp.bitwise_and(lax.broadcasted_iota(jnp.int32, (rows, u_scr.shape[1]), 0), GRID_W - 1)
    left = jnp.where(pos == 0, 0.0, u_scr[pad - 1:pad - 1 + rows, :])
    right = jnp.where(pos == GRID_W - 1, 0.0, u_scr[pad + 1:pad + 1 + rows, :])
    conv = cw_ref[0:1, :] * left + cw_ref[1:2, :] * u_scr[pad:pad + rows, :] + cw_ref[2:3, :] * right
    v = (scb_ref[...].astype(F32) * conv).astype(BF16)
    y_sc = jnp.dot(v, wsc_ref[...], preferred_element_type=F32)
    merged = m1_ref[...].astype(F32) + jax.nn.sigmoid(gsc_ref[...].astype(F32)) * y_sc
    out = jnp.dot(merged.astype(BF16), wo_ref[...], preferred_element_type=F32)
    x1 = x_ref[...] + g1_ref[0] * out
    x1_ref[...] = x1
    hq_ref[...] = (_rms(x1, n2_ref[...]) * (1.0 + sc2_ref[0]) + sh2_ref[0]).astype(hq_ref.dtype)


def _mixb(p3, m1, x2d, g1, sc_conv_w, w_sc, w_o, norm2_g, sh2, sc2, seq_len, scb_blk, gsc_blk):
    t, d = x2d.shape
    tm = min(256, t)
    per_batch = seq_len // tm
    bmap = lambda i: (i // per_batch, 0, 0)
    tile = lambda blk: pl.BlockSpec((tm, d), lambda i: (i, blk))
    ptile = lambda blk: pl.BlockSpec((None, tm, d), lambda i: (blk, i, 0))
    kern = functools.partial(_mixb_kernel, rows=tm)
    return pl.pallas_call(
        kern,
        grid=(t // tm,),
        in_specs=[ptile(scb_blk), ptile(scb_blk + 1), ptile(scb_blk + 2), ptile(gsc_blk),
                  tile(0), tile(0),
                  pl.BlockSpec((1, 1, d), bmap),
                  pl.BlockSpec(sc_conv_w.shape, lambda i: (0, 0)),
                  _resident((d, d), lambda i: (0, 0)),
                  _resident((d, d), lambda i: (0, 0)),
                  pl.BlockSpec((1, d), lambda i: (0, 0)),
                  pl.BlockSpec((1, 1, d), bmap),
                  pl.BlockSpec((1, 1, d), bmap)],
        out_specs=[tile(0), tile(0)],
        out_shape=[jax.ShapeDtypeStruct((t, d), F32), jax.ShapeDtypeStruct((t, d), BF16)],
        scratch_shapes=[pltpu.VMEM((tm + 16, d), F32)],
        compiler_params=_cparams(1),
        name="mixb",
    )(p3, p3, p3, p3, m1, x2d, g1, sc_conv_w, w_sc, w_o, norm2_g.reshape(1, d), sh2, sc2)


def _top_values(cur, n, sv_ref):
    for it in range(n):
        m = jnp.max(cur, axis=0, keepdims=True)
        sv_ref[it:it + 1, :] = m
        cur = jnp.where(cur >= m, NEG, cur)


def _peera_kernel(hq_ref, wq_ref, keys_ref, thr_ref, coef_ref, s2_ref, e2_ref,
                  sv1_scr, sv2_scr, cand_scr, svc_scr):
    k = PEER_TOPK
    sub = 8
    n_heads = keys_ref.shape[0]
    dk = keys_ref.shape[3]
    q_t = _nt_dot(wq_ref[...], hq_ref[...])
    row = lax.broadcasted_iota(jnp.int32, (sub, q_t.shape[1]), 0)
    for h in range(n_heads):
        s_t = []
        for s in range(2):
            r0 = (2 * h + s) * dk
            s_t.append(jnp.dot(keys_ref[h, s], q_t[r0:r0 + dk, :].astype(BF16),
                               preferred_element_type=F32))
        _top_values(s_t[0], k + 1, sv1_scr)
        _top_values(s_t[1], k + 1, sv2_scr)
        cand_scr[0:k, :] = sv1_scr[0:k, :] + sv2_scr[0:1, :]
        for b in range(1, sub):
            blk = sv1_scr[0:sub, :] + sv2_scr[b:b + 1, :]
            cand_scr[k + (b - 1) * sub:k + b * sub, :] = jnp.where(row < k // (b + 1), blk, NEG)
        cand_scr[k + (sub - 1) * sub:k + sub * sub, :] = sv2_scr[sub:k, :] + sv1_scr[0:1, :]
        cand = cand_scr[...]
        _top_values(cand, k + 1, svc_scr)
        c_next = jnp.maximum(svc_scr[k:k + 1, :],
                             jnp.maximum(sv1_scr[k:k + 1, :] + sv2_scr[0:1, :],
                                         sv1_scr[0:1, :] + sv2_scr[k:k + 1, :]))
        tau = 0.5 * (svc_scr[k - 1:k, :] + c_next)
        top = sv1_scr[0:1, :] + sv2_scr[0:1, :]
        z = jnp.sum(jnp.where(cand > tau, jnp.exp(cand - top), 0.0), axis=0, keepdims=True)
        thr_ref[h] = tau - s_t[0]
        coef_ref[h] = jnp.exp(s_t[0] - sv1_scr[0:1, :]) / z
        s2_ref[h] = s_t[1]
        e2_ref[h] = jnp.exp(s_t[1] - sv2_scr[0:1, :])


def _peera(hq, wq_t, keys):
    t, d = hq.shape
    n_heads, _, n_keys, _ = keys.shape
    tt = min(512, t)
    k = PEER_TOPK
    n_cand = k + 64
    out = jax.ShapeDtypeStruct((n_heads, n_keys, t), F32)
    ospec = pl.BlockSpec((n_heads, n_keys, tt), lambda i: (0, 0, i))
    return pl.pallas_call(
        _peera_kernel,
        grid=(t // tt,),
        in_specs=[pl.BlockSpec((tt, d), lambda i: (i, 0)),
                  _resident(wq_t.shape, lambda i: (0, 0)),
                  pl.BlockSpec(keys.shape, lambda i: (0, 0, 0, 0))],
        out_specs=[ospec] * 4,
        out_shape=[out] * 4,
        scratch_shapes=[pltpu.VMEM((k + 8, tt), F32), pltpu.VMEM((k + 8, tt), F32),
                        pltpu.VMEM((n_cand, tt), F32), pltpu.VMEM((k + 8, tt), F32)],
        compiler_params=_cparams(1),
        name="peera",
    )(hq, wq_t, keys)


def _peerb_kernel(hq_ref, thr_ref, coef_ref, s2_ref, e2_ref, u_ref, vt_ref, x1_ref, g2_ref, fg_ref,
                  o_ref, acc_scr, a0_scr, a1_scr, wg_new, *, n_eb):
    s = pl.program_id(0)
    blk_v = lax.rem(jnp.maximum(s - 1, 0), n_eb)
    n_heads, n_keys, _ = s2_ref.shape

    @pl.when(s == 0)
    def _():
        a1_scr[...] = jnp.zeros_like(a1_scr)

    @pl.when(blk_v == 0)
    def _():
        acc_scr[...] = jnp.zeros_like(acc_scr)

    def stages(a_new, a_old):
        half = jnp.where(s >= 1, 0.5, 0.0)
        n_i = thr_ref.shape[1]

        def gated_weights(sub):
            lanes = slice(sub * LANE, (sub + 1) * LANE)
            n_jc = n_keys // BF16_ROWS
            for ii in range(n_i):
                w = [None] * n_jc
                for h in range(n_heads):
                    thr = jnp.broadcast_to(thr_ref[h, ii:ii + 1, lanes], (BF16_ROWS, LANE))
                    coef = jnp.broadcast_to(coef_ref[h, ii:ii + 1, lanes], (BF16_ROWS, LANE))
                    for jc in range(n_jc):
                        jrows = slice(jc * BF16_ROWS, (jc + 1) * BF16_ROWS)
                        term = coef * jnp.where(s2_ref[h, jrows, lanes] >= thr, e2_ref[h, jrows, lanes], 0.0)
                        w[jc] = term if w[jc] is None else w[jc] + term
                for jc in range(n_jc):
                    rows = slice(ii * n_keys + jc * BF16_ROWS, ii * n_keys + (jc + 1) * BF16_ROWS)
                    act = a_old[rows, lanes]
                    gelu = (half * act) * (1.0 + lax.erf(act * (1.0 / math.sqrt(2.0))))
                    wg_new[rows, lanes] = (w[jc] * gelu).astype(BF16)

        tt = s2_ref.shape[2]
        piece = min(MXU_WIDTH, tt)
        subs = piece // LANE
        for pc in range(tt // piece):
            cols = slice(pc * piece, (pc + 1) * piece)
            for sub in range(pc * subs, pc * subs + (subs + 1) // 2):
                gated_weights(sub)
            a_new[:, cols] = _nt_dot(u_ref[...], hq_ref[cols, :])
            for sub in range(pc * subs + (subs + 1) // 2, (pc + 1) * subs):
                gated_weights(sub)
            acc_scr[:, cols] += jnp.dot(vt_ref[...], wg_new[:, cols], preferred_element_type=F32)

    @pl.when(s % 2 == 0)
    def _():
        stages(a0_scr, a1_scr)

    @pl.when(s % 2 == 1)
    def _():
        stages(a1_scr, a0_scr)

    @pl.when((s >= 1) & (blk_v == n_eb - 1))
    def _():
        x2 = x1_ref[...] + g2_ref[0] * acc_scr[...].T
        o_ref[...] = _rms(x2, fg_ref[...])


def _peerb(hq, thr, coef, s2, e2, u, v, x1, g2, final_g, seq_len):
    t, d = hq.shape
    n_heads, n_keys, _ = s2.shape
    n_exp = u.shape[0]
    tt = min(512, t)
    i_blk = 8
    et = i_blk * n_keys
    per_batch = seq_len // tt
    n_eb = n_exp // et
    v_t = v.reshape(n_eb, et, d).transpose(0, 2, 1)
    n_tiles = t // tt
    tile_a = lambda s: jnp.minimum(s // n_eb, n_tiles - 1)
    blk_a = lambda s: s % n_eb
    tile_v = lambda s: jnp.maximum(s - 1, 0) // n_eb
    blk_v = lambda s: jnp.maximum(s - 1, 0) % n_eb
    stat = pl.BlockSpec((n_heads, n_keys, tt), lambda s: (0, 0, tile_v(s)))
    rowstat = pl.BlockSpec((n_heads, i_blk, tt), lambda s: (0, blk_v(s), tile_v(s)))
    kern = functools.partial(_peerb_kernel, n_eb=n_eb)
    return pl.pallas_call(
        kern,
        grid=(n_tiles * n_eb + 1,),
        in_specs=[pl.BlockSpec((tt, d), lambda s: (tile_a(s), 0)),
                  rowstat, rowstat, stat, stat,
                  pl.BlockSpec((et, d), lambda s: (blk_a(s), 0)),
                  pl.BlockSpec((None, d, et), lambda s: (blk_v(s), 0, 0)),
                  pl.BlockSpec((tt, d), lambda s: (tile_v(s), 0), pipeline_mode=pl.Buffered(1)),
                  pl.BlockSpec((1, 1, d), lambda s: (tile_v(s) // per_batch, 0, 0)),
                  pl.BlockSpec((1, d), lambda s: (0, 0))],
        out_specs=pl.BlockSpec((tt, d), lambda s: (tile_v(s), 0)),
        out_shape=jax.ShapeDtypeStruct((t, d), F32),
        scratch_shapes=[pltpu.VMEM((d, tt), F32),
                        pltpu.VMEM((et, tt), F32), pltpu.VMEM((et, tt), F32),
                        pltpu.VMEM((et, tt), BF16)],
        compiler_params=_cparams(1),
        name="peerb",
    )(hq, thr, coef, s2, e2, u, v_t, x1, g2, final_g.reshape(1, d))


def kernel(x, c, ctx, c_ctx, w_ada, b_ada, norm1_g, norm2_g, w_in, ssd_conv_w, ssd_conv_b, ssd_dt_bias,
           ssd_A_log, ssd_D, ssd_norm_g, ssd_w_out, sc_conv_w, sc_w_out, w_o, peer_w_q, peer_keys,
           peer_u, peer_v, final_g):
    batch, seq_len, d = x.shape
    ctx_len = ctx.shape[1]
    depth = w_in.shape[0]
    assert depth == 1, "single-layer block"
    heads = ssd_D.shape[-1]
    d_ssd = ssd_norm_g.shape[-1]
    n_xbc = ssd_conv_w.shape[-1]
    gn = (n_xbc - d_ssd) // 2
    groups = gn // SSD_STATE
    d_sc = sc_conv_w.shape[-1]
    assert d_ssd // heads == LANE // 2 and heads % (2 * groups) == 0 and 2 * heads <= LANE
    assert d_sc == d and n_xbc % d == 0 and d_ssd % d == 0 and d % LANE == 0
    assert seq_len % SSD_CHUNK == 0 and ctx_len % SSD_CHUNK == 0

    off_dt = n_xbc
    off_z = off_dt + 2 * heads
    off_sc = off_z + d_ssd
    off_gate = off_sc + 3 * d_sc
    w_in0 = w_in[0]
    w_main = jnp.concatenate([w_in0[:, off_z:off_sc], w_in0[:, :off_dt], w_in0[:, off_sc:]],
                             axis=1).astype(BF16)
    w_main = w_main.reshape(d, -1, d).transpose(1, 0, 2)
    w_dt = jnp.pad(w_in0[:, off_dt:off_z], ((0, 0), (0, LANE - 2 * heads))).astype(BF16)
    dt_bias = jnp.pad(ssd_dt_bias[0].reshape(1, 2 * heads), ((0, 0), (0, LANE - 2 * heads)))
    a_log = jnp.pad(ssd_A_log[0].reshape(1, 2 * heads).astype(F32), ((0, 0), (0, LANE - 2 * heads)))
    xbc_blk = d_ssd // d
    scb_blk = (d_ssd + n_xbc) // d
    gate_blk = scb_blk + 3
    assert off_gate - off_sc == 3 * d

    rows = -(-(batch + 1) // 8) * 8
    cvec = jnp.zeros((rows, d), F32).at[:batch].set(c).at[batch].set(c_ctx)
    mod = _ada(cvec, w_ada[0], b_ada[0])
    chunk = lambda r, i: mod[r, i * d:(i + 1) * d]
    lat = lambda i: mod[:batch, i * d:(i + 1) * d].reshape(batch, 1, d)
    sh1, sc1, g1, sh2, sc2, g2 = (lat(i) for i in range(6))
    sh_c = jnp.broadcast_to(chunk(batch, 0), (batch, 1, d))
    sc_c = jnp.broadcast_to(chunk(batch, 1), (batch, 1, d))

    conv_w = ssd_conv_w[0]
    conv_b = ssd_conv_b[0]
    p_c, dt_c = _inproj(ctx.reshape(batch * ctx_len, d), norm1_g[0], sh_c, sc_c,
                        w_main[xbc_blk:xbc_blk + n_xbc // d], w_dt, ctx_len)
    x3_c = _ssd_conv(p_c, conv_w, conv_b, batch, ctx_len, n_xbc, 0)
    dt_c = dt_c.reshape(batch, ctx_len, LANE)
    zero_state = jnp.zeros((batch, heads // 2, LANE, SSD_STATE), F32)
    _, hf_c = _ssd_scan(x3_c, dt_c, dt_bias, a_log, zero_state, heads, groups, False)
    _, hb_c = _ssd_scan(x3_c, dt_c, dt_bias, a_log, zero_state, heads, groups, True)

    x2d = x.reshape(batch * seq_len, d)
    p, dt_l = _inproj(x2d, norm1_g[0], sh1, sc1, w_main, w_dt, seq_len)
    x3 = _ssd_conv(p, conv_w, conv_b, batch, seq_len, n_xbc, xbc_blk)
    dt_l = dt_l.reshape(batch, seq_len, LANE)
    drow = jnp.repeat(ssd_D[0], LANE // 2).reshape(heads // 2, LANE)
    y_f, _ = _ssd_scan(x3, dt_l, dt_bias, a_log, hf_c, heads, groups, False)
    y, _ = _ssd_scan(x3, dt_l, dt_bias, a_log, hb_c, heads, groups, True, drow=drow, yprev=y_f)

    m1 = _mixa(y.reshape(batch * seq_len, d_ssd), p, ssd_norm_g[0], ssd_w_out[0].astype(BF16), gate_blk)
    x1, hq = _mixb(p, m1, x2d, g1, sc_conv_w[0], sc_w_out[0].astype(BF16), w_o[0].astype(BF16),
                   norm2_g[0], sh2, sc2, seq_len, scb_blk, gate_blk + 1)

    thr, coef, s2, e2 = _peera(hq, peer_w_q[0].T.astype(BF16), peer_keys[0].astype(BF16))
    out = _peerb(hq, thr, coef, s2, e2, peer_u[0].astype(BF16), peer_v[0].astype(BF16), x1, g2,
                 final_g, seq_len)
    return out.reshape(batch, seq_len, d)
```

```python
import functools
import math

import jax
import jax.numpy as jnp
from jax import lax
from jax.experimental import pallas as pl
from jax.experimental.pallas import tpu as pltpu

F32 = jnp.float32
BF16 = jnp.bfloat16

LANE = 128
BF16_ROWS = 16
MXU_WIDTH = 256
VMEM_LIMIT_BYTES = 56 * 1024 * 1024

EPS = 1e-6
SSD_CHUNK = 128
SSD_STATE = 128
GRID_W = 64
PEER_TOPK = 16
NEG = -1e30

HIGHEST = lax.Precision.HIGHEST


def _cparams(n_axes, flags=None):
    return pltpu.CompilerParams(dimension_semantics=("arbitrary",) * n_axes,
                                vmem_limit_bytes=VMEM_LIMIT_BYTES, flags=flags)


def _resident(block_shape, index_map):
    return pl.BlockSpec(block_shape, index_map, pipeline_mode=pl.Buffered(1))


def _silu(v):
    return v * jax.nn.sigmoid(v)


def _rms(v, g):
    return v * lax.rsqrt(jnp.mean(v * v, axis=-1, keepdims=True) + EPS) * g


def _nt_dot(a, b):
    return lax.dot_general(a, b, (((1,), (1,)), ((), ())), preferred_element_type=F32)


def _ada_kernel(c_ref, w_ref, b_ref, o_ref):
    s = _silu(c_ref[...])
    o_ref[...] = jnp.dot(s, w_ref[...], preferred_element_type=F32, precision=HIGHEST) + b_ref[...]


def _ada(cvec, w, b):
    rows, d = cvec.shape
    n = w.shape[1]
    tn = 1024 if n % 1024 == 0 else d
    return pl.pallas_call(
        _ada_kernel,
        grid=(n // tn,),
        in_specs=[pl.BlockSpec((rows, d), lambda j: (0, 0)),
                  pl.BlockSpec((d, tn), lambda j: (0, j)),
                  pl.BlockSpec((1, tn), lambda j: (0, j))],
        out_specs=pl.BlockSpec((rows, tn), lambda j: (0, j)),
        out_shape=jax.ShapeDtypeStruct((rows, n), F32),
        compiler_params=_cparams(1),
        name="ada",
    )(cvec, w, b.reshape(1, n))


def _inproj_kernel(x_ref, g_ref, sh_ref, sc_ref, w_ref, wdt_ref, p_ref, dt_ref, h_scr):
    @pl.when(pl.program_id(1) == 0)
    def _():
        h = _rms(x_ref[...], g_ref[...]) * (1.0 + sc_ref[0]) + sh_ref[0]
        hb = h.astype(BF16)
        h_scr[...] = hb
        dt_ref[...] = jnp.dot(hb, wdt_ref[...], preferred_element_type=F32)

    p_ref[...] = jnp.dot(h_scr[...], w_ref[...], preferred_element_type=F32).astype(p_ref.dtype)


def _inproj(x2d, norm_g, shift, scale, w_main, w_dt, seq_len):
    t, d = x2d.shape
    n_blk, _, tn = w_main.shape
    tm = min(512, seq_len)
    per_batch = seq_len // tm
    return pl.pallas_call(
        _inproj_kernel,
        grid=(t // tm, n_blk),
        in_specs=[pl.BlockSpec((tm, d), lambda i, j: (i, 0)),
                  pl.BlockSpec((1, d), lambda i, j: (0, 0)),
                  pl.BlockSpec((1, 1, d), lambda i, j: (i // per_batch, 0, 0)),
                  pl.BlockSpec((1, 1, d), lambda i, j: (i // per_batch, 0, 0)),
                  pl.BlockSpec((None, d, tn), lambda i, j: (j, 0, 0)),
                  pl.BlockSpec((d, LANE), lambda i, j: (0, 0))],
        out_specs=[pl.BlockSpec((None, tm, tn), lambda i, j: (j, i, 0)),
                   pl.BlockSpec((tm, LANE), lambda i, j: (i, 0))],
        out_shape=[jax.ShapeDtypeStruct((n_blk, t, tn), BF16),
                   jax.ShapeDtypeStruct((t, LANE), F32)],
        scratch_shapes=[pltpu.VMEM((tm, d), BF16)],
        compiler_params=_cparams(2),
        name="inproj",
    )(x2d, norm_g.reshape(1, d), shift, scale, w_main, w_dt)


def _conv_kernel(main_ref, prev_ref, next_ref, w_ref, b_ref, o_ref, e_scr, st_scr, *, rows, n_rb, taps):
    rb = pl.program_id(1)
    halo = BF16_ROWS
    pad = taps // 2
    half = rows // 2
    for t in range(o_ref.shape[1]):
        lanes = slice(t * LANE, (t + 1) * LANE)
        e_scr[t, halo:halo + rows, :] = main_ref[:, lanes].astype(F32)
        e_scr[t, 0:halo, :] = jnp.where(rb > 0, prev_ref[:, lanes].astype(F32), 0.0)
        e_scr[t, halo + rows:2 * halo + rows, :] = jnp.where(rb < n_rb - 1,
                                                             next_ref[:, lanes].astype(F32), 0.0)
    for t in range(o_ref.shape[1]):
        lanes = slice(t * LANE, (t + 1) * LANE)
        for par in range(2):
            acc = b_ref[:, lanes]
            for k in range(taps):
                acc = acc + w_ref[k:k + 1, lanes] * e_scr[t, pl.ds(halo - pad + par + k, half, stride=2), :]
            st_scr[t, pl.ds(par, half, stride=2), :] = _silu(acc)
        o_ref[0, t] = st_scr[t].astype(o_ref.dtype)


def _ssd_conv(p3, conv_w, conv_b, batch, seq_len, n_xbc, col_blk):
    tc = p3.shape[2]
    taps = conv_w.shape[0]
    rows = min(512, seq_len)
    n_rb = seq_len // rows
    n_col = n_xbc // tc
    halo = BF16_ROWS
    blocks_per_rb = rows // halo
    last_halo_block = batch * seq_len // halo - 1
    kern = functools.partial(_conv_kernel, rows=rows, n_rb=n_rb, taps=taps)
    return pl.pallas_call(
        kern,
        grid=(batch, n_rb, n_col),
        in_specs=[
            pl.BlockSpec((None, rows, tc), lambda b, r, j: (j + col_blk, b * n_rb + r, 0)),
            pl.BlockSpec((None, halo, tc),
                         lambda b, r, j: (j + col_blk, jnp.maximum((b * n_rb + r) * blocks_per_rb - 1, 0), 0)),
            pl.BlockSpec((None, halo, tc),
                         lambda b, r, j: (j + col_blk,
                                          jnp.minimum((b * n_rb + r + 1) * blocks_per_rb, last_halo_block), 0)),
            pl.BlockSpec((taps, tc), lambda b, r, j: (0, j)),
            pl.BlockSpec((1, tc), lambda b, r, j: (0, j)),
        ],
        out_specs=pl.BlockSpec((1, tc // LANE, rows, LANE), lambda b, r, j: (b, j, r, 0)),
        out_shape=jax.ShapeDtypeStruct((batch, n_xbc // LANE, seq_len, LANE), BF16),
        scratch_shapes=[pltpu.VMEM((tc // LANE, rows + 2 * halo, LANE), F32),
                        pltpu.VMEM((tc // LANE, rows, LANE), F32)],
        compiler_params=_cparams(3),
        name="ssdconv",
    )(p3, p3, p3, conv_w, conv_b.reshape(1, n_xbc))


def _ssd_kernel(*refs, reverse, add_prev, n_chunks, heads, groups):
    if add_prev:
        (x3_ref, dtr_ref, bias_ref, alog_ref, h0_ref, drow_ref, yprev_ref,
         y_ref, hfin_ref, h_scr) = refs
    else:
        x3_ref, dtr_ref, bias_ref, alog_ref, h0_ref, y_ref, hfin_ref, h_scr = refs
    q = SSD_CHUNK
    n_pairs = heads // 2
    pairs_per_group = n_pairs // groups
    col0 = heads if reverse else 0
    step = pl.program_id(1)

    @pl.when(step == 0)
    def _():
        h_scr[...] = h0_ref[0]

    pre = dtr_ref[0] + bias_ref[...]
    dt = jnp.maximum(pre, 0.0) + jnp.log1p(jnp.exp(-jnp.abs(pre)))
    a = dt * (-jnp.exp(alog_ref[...]))
    ri = lax.broadcasted_iota(jnp.int32, (q, q), 0)
    ci = lax.broadcasted_iota(jnp.int32, (q, q), 1)
    tri = (ci >= ri) if reverse else (ci <= ri)
    tri_t = (ri >= ci) if reverse else (ri <= ci)
    acum = jnp.dot(tri.astype(F32), a, preferred_element_type=F32, precision=HIGHEST)
    a_t = a.T
    dt_t = dt.T
    acum_t = jnp.dot(a_t, tri_t.astype(F32), preferred_element_type=F32, precision=HIGHEST)
    tot = jnp.dot(a_t, jnp.ones((q, q), F32), preferred_element_type=F32, precision=HIGHEST)
    w_t = dt_t * jnp.exp(tot - acum_t)
    dec_t = jnp.exp(tot)
    low_lanes = ci < (LANE // 2)
    low_rows = ri < (LANE // 2)

    cb = None
    for kp in range(n_pairs):
        g = kp // pairs_per_group
        b_g = x3_ref[0, n_pairs + g]
        c_g = x3_ref[0, n_pairs + groups + g]
        if kp % pairs_per_group == 0:
            cb = _nt_dot(c_g, b_g)
            c_f = c_g.astype(F32)
        ms, cds = [], []
        for hc in (col0 + 2 * kp, col0 + 2 * kp + 1):
            col = jnp.broadcast_to(acum[:, hc:hc + 1], (q, q))
            seg = col - acum_t[hc:hc + 1, :]
            decay = jnp.exp(jnp.where(tri, seg, NEG))
            ms.append((cb * decay * dt_t[hc:hc + 1, :]).astype(BF16))
            cds.append((c_f * jnp.exp(col)).astype(BF16))
        xp = x3_ref[0, kp].astype(F32)
        rhs = jnp.concatenate([jnp.where(low_lanes, xp, 0.0), jnp.where(low_lanes, 0.0, xp)],
                              axis=0).astype(BF16)
        y = jnp.dot(jnp.concatenate(ms, axis=1), rhs, preferred_element_type=F32)
        hp = h_scr[kp]
        rhs2 = jnp.concatenate([jnp.where(low_rows, hp, 0.0), jnp.where(low_rows, 0.0, hp)],
                               axis=1).astype(BF16)
        y = y + _nt_dot(jnp.concatenate(cds, axis=1), rhs2)
        if add_prev:
            y = y + drow_ref[kp:kp + 1, :] * xp + yprev_ref[0, :, kp * LANE:(kp + 1) * LANE].astype(F32)
        y_ref[0, :, kp * LANE:(kp + 1) * LANE] = y.astype(y_ref.dtype)
        h1, h2 = col0 + 2 * kp, col0 + 2 * kp + 1
        w_sel = jnp.where(low_rows, w_t[h1:h1 + 1, :], w_t[h2:h2 + 1, :])
        d_sel = jnp.where(low_rows, dec_t[h1:h1 + 1, :], dec_t[h2:h2 + 1, :])
        xw_t = (xp.T * w_sel).astype(BF16)
        h_scr[kp] = d_sel * hp + jnp.dot(xw_t, b_g, preferred_element_type=F32)

    @pl.when(step == n_chunks - 1)
    def _():
        hfin_ref[0] = h_scr[...]


def _ssd_scan(x3, dtraw, dt_bias, a_log, h0, heads, groups, reverse, drow=None, yprev=None):
    batch, n_tiles, seq_len, _ = x3.shape
    n_chunks = seq_len // SSD_CHUNK
    n_pairs = heads // 2
    d_ssd = n_pairs * LANE
    add_prev = yprev is not None
    if reverse:
        chunk = lambda k: n_chunks - 1 - k
    else:
        chunk = lambda k: k
    in_specs = [
        pl.BlockSpec((1, n_tiles, SSD_CHUNK, LANE), lambda b, k: (b, 0, chunk(k), 0)),
        pl.BlockSpec((1, SSD_CHUNK, LANE), lambda b, k: (b, chunk(k), 0)),
        pl.BlockSpec((1, LANE), lambda b, k: (0, 0)),
        pl.BlockSpec((1, LANE), lambda b, k: (0, 0)),
        pl.BlockSpec((1, n_pairs, LANE, SSD_STATE), lambda b, k: (b, 0, 0, 0)),
    ]
    args = [x3, dtraw, dt_bias, a_log, h0]
    if add_prev:
        in_specs += [pl.BlockSpec((n_pairs, LANE), lambda b, k: (0, 0)),
                     pl.BlockSpec((1, SSD_CHUNK, d_ssd), lambda b, k: (b, chunk(k), 0))]
        args += [drow, yprev]
    kern = functools.partial(_ssd_kernel, reverse=reverse, add_prev=add_prev, n_chunks=n_chunks,
                             heads=heads, groups=groups)
    return pl.pallas_call(
        kern,
        grid=(batch, n_chunks),
        in_specs=in_specs,
        out_specs=[pl.BlockSpec((1, SSD_CHUNK, d_ssd), lambda b, k: (b, chunk(k), 0)),
                   pl.BlockSpec((1, n_pairs, LANE, SSD_STATE), lambda b, k: (b, 0, 0, 0))],
        out_shape=[jax.ShapeDtypeStruct((batch, seq_len, d_ssd), BF16),
                   jax.ShapeDtypeStruct((batch, n_pairs, LANE, SSD_STATE), F32)],
        scratch_shapes=[pltpu.VMEM((n_pairs, LANE, SSD_STATE), F32)],
        compiler_params=_cparams(2),
        name="ssd_bwd" if reverse else "ssd_fwd",
    )(*args)


def _mixa_kernel(*refs):
    y_ref, *z_refs, gate_ref, ng_ref, w_ref, o_ref = refs
    z = jnp.concatenate([r[...] for r in z_refs], axis=1).astype(F32)
    v = _rms(y_ref[...].astype(F32) * _silu(z), ng_ref[...])
    o = jnp.dot(v.astype(BF16), w_ref[...], preferred_element_type=F32)
    o_ref[...] = (jax.nn.sigmoid(gate_ref[...].astype(F32)) * o).astype(o_ref.dtype)


def _mixa(y2d, p3, norm_g, w_out, gate_blk):
    t, d_ssd = y2d.shape
    d = w_out.shape[1]
    tm = min(256, t)
    n_z = d_ssd // d
    return pl.pallas_call(
        _mixa_kernel,
        grid=(t // tm,),
        in_specs=[pl.BlockSpec((tm, d_ssd), lambda i: (i, 0))]
                 + [pl.BlockSpec((None, tm, d), functools.partial(lambda k, i: (k, i, 0), k))
                    for k in range(n_z)]
                 + [pl.BlockSpec((None, tm, d), lambda i: (gate_blk, i, 0)),
                  pl.BlockSpec((1, d_ssd), lambda i: (0, 0)),
                  _resident((d_ssd, d), lambda i: (0, 0))],
        out_specs=pl.BlockSpec((tm, d), lambda i: (i, 0)),
        out_shape=jax.ShapeDtypeStruct((t, d), BF16),
        compiler_params=_cparams(1),
        name="mixa",
    )(y2d, *([p3] * (n_z + 1)), norm_g.reshape(1, d_ssd), w_out)


def _mixb_kernel(scb_ref, scc_ref, scx_ref, gsc_ref, m1_ref, x_ref, g1_ref, cw_ref, wsc_ref, wo_ref,
                 n2_ref, sh2_ref, sc2_ref, x1_ref, hq_ref, u_scr, st_scr, *, rows):
    pad = 8
    half = rows // 2
    n_slab = u_scr.shape[0]
    pair = jnp.bitwise_and(lax.broadcasted_iota(jnp.int32, (half, LANE), 0), GRID_W // 2 - 1)
    first = pair == 0
    last = pair == GRID_W // 2 - 1
    for t in range(n_slab):
        lanes = slice(t * LANE, (t + 1) * LANE)
        u_scr[t, 0:pad, :] = jnp.zeros((pad, LANE), F32)
        u_scr[t, pad + rows:2 * pad + rows, :] = jnp.zeros((pad, LANE), F32)
        u_scr[t, pad:pad + rows, :] = scc_ref[:, lanes].astype(F32) * scx_ref[:, lanes].astype(F32)
    for t in range(n_slab):
        lanes = slice(t * LANE, (t + 1) * LANE)
        w0, w1, w2 = cw_ref[0:1, lanes], cw_ref[1:2, lanes], cw_ref[2:3, lanes]
        tap = lambda off: u_scr[t, pl.ds(pad + off, half, stride=2), :]
        st_scr[t, pl.ds(0, half, stride=2), :] = (w0 * jnp.where(first, 0.0, tap(-1)) + w1 * tap(0)
                                                  + w2 * tap(1))
        st_scr[t, pl.ds(1, half, stride=2), :] = (w0 * tap(0) + w1 * tap(1)
                                                  + w2 * jnp.where(last, 0.0, tap(2)))
    conv = jnp.concatenate([st_scr[t] for t in range(n_slab)], axis=1)
    v = (scb_ref[...].astype(F32) * conv).astype(BF16)
    y_sc = jnp.dot(v, wsc_ref[...], preferred_element_type=F32)
    merged = m1_ref[...].astype(F32) + jax.nn.sigmoid(gsc_ref[...].astype(F32)) * y_sc
    out = jnp.dot(merged.astype(BF16), wo_ref[...], preferred_element_type=F32)
    x1 = x_ref[...] + g1_ref[0] * out
    x1_ref[...] = x1
    hq_ref[...] = (_rms(x1, n2_ref[...]) * (1.0 + sc2_ref[0]) + sh2_ref[0]).astype(hq_ref.dtype)


def _mixb(p3, m1, x2d, g1, sc_conv_w, w_sc, w_o, norm2_g, sh2, sc2, seq_len, scb_blk, gsc_blk):
    t, d = x2d.shape
    tm = min(256, t)
    per_batch = seq_len // tm
    bmap = lambda i: (i // per_batch, 0, 0)
    tile = lambda blk: pl.BlockSpec((tm, d), lambda i: (i, blk))
    ptile = lambda blk: pl.BlockSpec((None, tm, d), lambda i: (blk, i, 0))
    kern = functools.partial(_mixb_kernel, rows=tm)
    return pl.pallas_call(
        kern,
        grid=(t // tm,),
        in_specs=[ptile(scb_blk), ptile(scb_blk + 1), ptile(scb_blk + 2), ptile(gsc_blk),
                  tile(0), tile(0),
                  pl.BlockSpec((1, 1, d), bmap),
                  pl.BlockSpec(sc_conv_w.shape, lambda i: (0, 0)),
                  _resident((d, d), lambda i: (0, 0)),
                  _resident((d, d), lambda i: (0, 0)),
                  pl.BlockSpec((1, d), lambda i: (0, 0)),
                  pl.BlockSpec((1, 1, d), bmap),
                  pl.BlockSpec((1, 1, d), bmap)],
        out_specs=[tile(0), tile(0)],
        out_shape=[jax.ShapeDtypeStruct((t, d), F32), jax.ShapeDtypeStruct((t, d), BF16)],
        scratch_shapes=[pltpu.VMEM((d // LANE, tm + 16, LANE), F32),
                        pltpu.VMEM((d // LANE, tm, LANE), F32)],
        compiler_params=_cparams(1),
        name="mixb",
    )(p3, p3, p3, p3, m1, x2d, g1, sc_conv_w, w_sc, w_o, norm2_g.reshape(1, d), sh2, sc2)


def _top_values(cur, n, sv_ref):
    for it in range(n):
        m = jnp.max(cur, axis=0, keepdims=True)
        sv_ref[it:it + 1, :] = m
        cur = jnp.where(cur >= m, NEG, cur)


def _peera_kernel(hq_ref, wq_ref, keys_ref, thr_ref, coef_ref, s2_ref, e2_ref,
                  sv1_scr, sv2_scr, cand_scr, svc_scr):
    k = PEER_TOPK
    sub = 8
    n_heads = keys_ref.shape[0]
    dk = keys_ref.shape[3]
    q_t = _nt_dot(wq_ref[...], hq_ref[...])
    row = lax.broadcasted_iota(jnp.int32, (sub, q_t.shape[1]), 0)
    for h in range(n_heads):
        s_t = []
        for s in range(2):
            r0 = (2 * h + s) * dk
            s_t.append(jnp.dot(keys_ref[h, s], q_t[r0:r0 + dk, :].astype(BF16),
                               preferred_element_type=F32))
        _top_values(s_t[0], k + 1, sv1_scr)
        _top_values(s_t[1], k + 1, sv2_scr)
        cand_scr[0:k, :] = sv1_scr[0:k, :] + sv2_scr[0:1, :]
        for b in range(1, sub):
            blk = sv1_scr[0:sub, :] + sv2_scr[b:b + 1, :]
            cand_scr[k + (b - 1) * sub:k + b * sub, :] = jnp.where(row < k // (b + 1), blk, NEG)
        cand_scr[k + (sub - 1) * sub:k + sub * sub, :] = sv2_scr[sub:k, :] + sv1_scr[0:1, :]
        cand = cand_scr[...]
        _top_values(cand, k + 1, svc_scr)
        c_next = jnp.maximum(svc_scr[k:k + 1, :],
                             jnp.maximum(sv1_scr[k:k + 1, :] + sv2_scr[0:1, :],
                                         sv1_scr[0:1, :] + sv2_scr[k:k + 1, :]))
        tau = 0.5 * (svc_scr[k - 1:k, :] + c_next)
        top = sv1_scr[0:1, :] + sv2_scr[0:1, :]
        z = jnp.sum(jnp.where(cand > tau, jnp.exp(cand - top), 0.0), axis=0, keepdims=True)
        thr_ref[h] = tau - s_t[0]
        coef_ref[h] = jnp.exp(s_t[0] - sv1_scr[0:1, :]) / z
        s2_ref[h] = s_t[1]
        e2_ref[h] = jnp.exp(s_t[1] - sv2_scr[0:1, :])


def _peera(hq, wq_t, keys):
    t, d = hq.shape
    n_heads, _, n_keys, _ = keys.shape
    tt = min(512, t)
    k = PEER_TOPK
    n_cand = k + 64
    out = jax.ShapeDtypeStruct((n_heads, n_keys, t), F32)
    ospec = pl.BlockSpec((n_heads, n_keys, tt), lambda i: (0, 0, i))
    return pl.pallas_call(
        _peera_kernel,
        grid=(t // tt,),
        in_specs=[pl.BlockSpec((tt, d), lambda i: (i, 0)),
                  _resident(wq_t.shape, lambda i: (0, 0)),
                  pl.BlockSpec(keys.shape, lambda i: (0, 0, 0, 0))],
        out_specs=[ospec] * 4,
        out_shape=[out] * 4,
        scratch_shapes=[pltpu.VMEM((k + 8, tt), F32), pltpu.VMEM((k + 8, tt), F32),
                        pltpu.VMEM((n_cand, tt), F32), pltpu.VMEM((k + 8, tt), F32)],
        compiler_params=_cparams(1),
        name="peera",
    )(hq, wq_t, keys)


def _peerb_kernel(hq_ref, thr_ref, coef_ref, s2_ref, e2_ref, u_ref, vt_ref, x1_ref, g2_ref, fg_ref,
                  o_ref, acc_scr, a0_scr, a1_scr, wg_new, *, n_eb):
    s = pl.program_id(0)
    blk_v = lax.rem(jnp.maximum(s - 1, 0), n_eb)
    n_heads, n_keys, _ = s2_ref.shape

    @pl.when(s == 0)
    def _():
        a1_scr[...] = jnp.zeros_like(a1_scr)

    @pl.when(blk_v == 0)
    def _():
        acc_scr[...] = jnp.zeros_like(acc_scr)

    def stages(a_new, a_old):
        half = jnp.where(s >= 1, 0.5, 0.0)
        n_i = thr_ref.shape[1]

        def gated_weights(sub):
            lanes = slice(sub * LANE, (sub + 1) * LANE)
            n_jc = n_keys // BF16_ROWS
            for ii in range(n_i):
                w = [None] * n_jc
                for h in range(n_heads):
                    thr = jnp.broadcast_to(thr_ref[h, ii:ii + 1, lanes], (BF16_ROWS, LANE))
                    coef = jnp.broadcast_to(coef_ref[h, ii:ii + 1, lanes], (BF16_ROWS, LANE))
                    for jc in range(n_jc):
                        jrows = slice(jc * BF16_ROWS, (jc + 1) * BF16_ROWS)
                        term = coef * jnp.where(s2_ref[h, jrows, lanes] >= thr, e2_ref[h, jrows, lanes], 0.0)
                        w[jc] = term if w[jc] is None else w[jc] + term
                for jc in range(n_jc):
                    rows = slice(ii * n_keys + jc * BF16_ROWS, ii * n_keys + (jc + 1) * BF16_ROWS)
                    act = a_old[rows, lanes]
                    gelu = (half * act) * (1.0 + lax.erf(act * (1.0 / math.sqrt(2.0))))
                    wg_new[rows, lanes] = (w[jc] * gelu).astype(BF16)

        tt = s2_ref.shape[2]
        piece = min(MXU_WIDTH, tt)
        subs = piece // LANE
        for pc in range(tt // piece):
            cols = slice(pc * piece, (pc + 1) * piece)
            for sub in range(pc * subs, pc * subs + (subs + 1) // 2):
                gated_weights(sub)
            a_new[:, cols] = _nt_dot(u_ref[...], hq_ref[cols, :])
            for sub in range(pc * subs + (subs + 1) // 2, (pc + 1) * subs):
                gated_weights(sub)
            acc_scr[:, cols] += jnp.dot(vt_ref[...], wg_new[:, cols], preferred_element_type=F32)

    @pl.when(s % 2 == 0)
    def _():
        stages(a0_scr, a1_scr)

    @pl.when(s % 2 == 1)
    def _():
        stages(a1_scr, a0_scr)

    @pl.when((s >= 1) & (blk_v == n_eb - 1))
    def _():
        x2 = x1_ref[...] + g2_ref[0] * acc_scr[...].T
        o_ref[...] = _rms(x2, fg_ref[...])


def _peerb(hq, thr, coef, s2, e2, u, v, x1, g2, final_g, seq_len):
    t, d = hq.shape
    n_heads, n_keys, _ = s2.shape
    n_exp = u.shape[0]
    tt = min(512, t)
    i_blk = 8
    et = i_blk * n_keys
    per_batch = seq_len // tt
    n_eb = n_exp // et
    v_t = v.reshape(n_eb, et, d).transpose(0, 2, 1)
    n_tiles = t // tt
    tile_a = lambda s: jnp.minimum(s // n_eb, n_tiles - 1)
    blk_a = lambda s: s % n_eb
    tile_v = lambda s: jnp.maximum(s - 1, 0) // n_eb
    blk_v = lambda s: jnp.maximum(s - 1, 0) % n_eb
    stat = pl.BlockSpec((n_heads, n_keys, tt), lambda s: (0, 0, tile_v(s)))
    rowstat = pl.BlockSpec((n_heads, i_blk, tt), lambda s: (0, blk_v(s), tile_v(s)))
    kern = functools.partial(_peerb_kernel, n_eb=n_eb)
    return pl.pallas_call(
        kern,
        grid=(n_tiles * n_eb + 1,),
        in_specs=[pl.BlockSpec((tt, d), lambda s: (tile_a(s), 0)),
                  rowstat, rowstat, stat, stat,
                  pl.BlockSpec((et, d), lambda s: (blk_a(s), 0)),
                  pl.BlockSpec((None, d, et), lambda s: (blk_v(s), 0, 0)),
                  pl.BlockSpec((tt, d), lambda s: (tile_v(s), 0), pipeline_mode=pl.Buffered(1)),
                  pl.BlockSpec((1, 1, d), lambda s: (tile_v(s) // per_batch, 0, 0)),
                  pl.BlockSpec((1, d), lambda s: (0, 0))],
        out_specs=pl.BlockSpec((tt, d), lambda s: (tile_v(s), 0)),
        out_shape=jax.ShapeDtypeStruct((t, d), F32),
        scratch_shapes=[pltpu.VMEM((d, tt), F32),
                        pltpu.VMEM((et, tt), F32), pltpu.VMEM((et, tt), F32),
                        pltpu.VMEM((et, tt), BF16)],
        compiler_params=_cparams(1),
        name="peerb",
    )(hq, thr, coef, s2, e2, u, v_t, x1, g2, final_g.reshape(1, d))


def kernel(x, c, ctx, c_ctx, w_ada, b_ada, norm1_g, norm2_g, w_in, ssd_conv_w, ssd_conv_b, ssd_dt_bias,
           ssd_A_log, ssd_D, ssd_norm_g, ssd_w_out, sc_conv_w, sc_w_out, w_o, peer_w_q, peer_keys,
           peer_u, peer_v, final_g):
    batch, seq_len, d = x.shape
    ctx_len = ctx.shape[1]
    depth = w_in.shape[0]
    assert depth == 1, "single-layer block"
    heads = ssd_D.shape[-1]
    d_ssd = ssd_norm_g.shape[-1]
    n_xbc = ssd_conv_w.shape[-1]
    gn = (n_xbc - d_ssd) // 2
    groups = gn // SSD_STATE
    d_sc = sc_conv_w.shape[-1]
    assert d_ssd // heads == LANE // 2 and heads % (2 * groups) == 0 and 2 * heads <= LANE
    assert d_sc == d and n_xbc % d == 0 and d_ssd % d == 0 and d % LANE == 0
    assert seq_len % SSD_CHUNK == 0 and ctx_len % SSD_CHUNK == 0

    off_dt = n_xbc
    off_z = off_dt + 2 * heads
    off_sc = off_z + d_ssd
    off_gate = off_sc + 3 * d_sc
    w_in0 = w_in[0]
    w_main = jnp.concatenate([w_in0[:, off_z:off_sc], w_in0[:, :off_dt], w_in0[:, off_sc:]],
                             axis=1).astype(BF16)
    w_main = w_main.reshape(d, -1, d).transpose(1, 0, 2)
    w_dt = jnp.pad(w_in0[:, off_dt:off_z], ((0, 0), (0, LANE - 2 * heads))).astype(BF16)
    dt_bias = jnp.pad(ssd_dt_bias[0].reshape(1, 2 * heads), ((0, 0), (0, LANE - 2 * heads)))
    a_log = jnp.pad(ssd_A_log[0].reshape(1, 2 * heads).astype(F32), ((0, 0), (0, LANE - 2 * heads)))
    xbc_blk = d_ssd // d
    scb_blk = (d_ssd + n_xbc) // d
    gate_blk = scb_blk + 3
    assert off_gate - off_sc == 3 * d

    rows = -(-(batch + 1) // 8) * 8
    cvec = jnp.zeros((rows, d), F32).at[:batch].set(c).at[batch].set(c_ctx)
    mod = _ada(cvec, w_ada[0], b_ada[0])
    chunk = lambda r, i: mod[r, i * d:(i + 1) * d]
    lat = lambda i: mod[:batch, i * d:(i + 1) * d].reshape(batch, 1, d)
    sh1, sc1, g1, sh2, sc2, g2 = (lat(i) for i in range(6))
    sh_c = jnp.broadcast_to(chunk(batch, 0), (batch, 1, d))
    sc_c = jnp.broadcast_to(chunk(batch, 1), (batch, 1, d))

    conv_w = ssd_conv_w[0]
    conv_b = ssd_conv_b[0]
    p_c, dt_c = _inproj(ctx.reshape(batch * ctx_len, d), norm1_g[0], sh_c, sc_c,
                        w_main[xbc_blk:xbc_blk + n_xbc // d], w_dt, ctx_len)
    x3_c = _ssd_conv(p_c, conv_w, conv_b, batch, ctx_len, n_xbc, 0)
    dt_c = dt_c.reshape(batch, ctx_len, LANE)
    zero_state = jnp.zeros((batch, heads // 2, LANE, SSD_STATE), F32)
    _, hf_c = _ssd_scan(x3_c, dt_c, dt_bias, a_log, zero_state, heads, groups, False)
    _, hb_c = _ssd_scan(x3_c, dt_c, dt_bias, a_log, zero_state, heads, groups, True)

    x2d = x.reshape(batch * seq_len, d)
    p, dt_l = _inproj(x2d, norm1_g[0], sh1, sc1, w_main, w_dt, seq_len)
    x3 = _ssd_conv(p, conv_w, conv_b, batch, seq_len, n_xbc, xbc_blk)
    dt_l = dt_l.reshape(batch, seq_len, LANE)
    drow = jnp.repeat(ssd_D[0], LANE // 2).reshape(heads // 2, LANE)
    y_f, _ = _ssd_scan(x3, dt_l, dt_bias, a_log, hf_c, heads, groups, False)
    y, _ = _ssd_scan(x3, dt_l, dt_bias, a_log, hb_c, heads, groups, True, drow=drow, yprev=y_f)

    m1 = _mixa(y.reshape(batch * seq_len, d_ssd), p, ssd_norm_g[0], ssd_w_out[0].astype(BF16), gate_blk)
    x1, hq = _mixb(p, m1, x2d, g1, sc_conv_w[0], sc_w_out[0].astype(BF16), w_o[0].astype(BF16),
                   norm2_g[0], sh2, sc2, seq_len, scb_blk, gate_blk + 1)

    thr, coef, s2, e2 = _peera(hq, peer_w_q[0].T.astype(BF16), peer_keys[0].astype(BF16))
    out = _peerb(hq, thr, coef, s2, e2, peer_u[0].astype(BF16), peer_v[0].astype(BF16), x1, g2,
                 final_g, seq_len)
    return out.reshape(batch, seq_len, d)
```
